```python
import jax, jax.numpy as jnp
from jax import lax
import numpy as np

D_MODEL = 2048
BATCH = 16
SEQ = 256
DEPTH = 2
DEC_BATCH = 2
DEC_SEQ = 4096
PAST_LEN = 512

GRID_W = 64
HEAD_DIM = 128
N_BRANCH = 4
BRANCH_W = 512
NA_HEADS = 4
NA_MAX_ROWS = 8
NA_COLS = 16
NA_KEY_COLS = 2 * NA_COLS
WIN_HEADS = 4
WIN_KV_HEADS = 2
WIN_RADIUS = 128
WIN_BLOCK = 128
POOL_GROUPS = 4
POOL_GROUP_W = 128
POOL_WINDOWS = (2, 4, 8, 16)
FNET_GROUPS = 4
FNET_GROUP_W = 128
N_EXPERTS = 16
EXPERT_FF = 2 * D_MODEL
EC_CAPACITY_FACTOR = 2
ROPE_THETA = 10000.0
NORM_EPS = 1e-6
NEG_INF = -1e30
IN_SIZES = (NA_HEADS * HEAD_DIM, NA_HEADS * HEAD_DIM, NA_HEADS * HEAD_DIM, POOL_GROUPS * POOL_GROUP_W,
            WIN_HEADS * HEAD_DIM, WIN_KV_HEADS * HEAD_DIM, WIN_KV_HEADS * HEAD_DIM,
            FNET_GROUPS * FNET_GROUP_W, N_BRANCH * D_MODEL)
IN_COLS = sum(IN_SIZES)

kernel_name = 'hybrid_dit_prefix_ctx_step'


def rmsnorm(x, g):
    xf = x.astype(jnp.float32)
    y = xf * lax.rsqrt(jnp.mean(xf * xf, axis=-1, keepdims=True) + NORM_EPS)
    return (y * g.astype(jnp.float32)).astype(x.dtype)


def modulation(cvec, w_ada, b_ada):
    m = (jax.nn.silu(cvec) @ w_ada + b_ada)[:, None, :]
    return jnp.split(m, 6, axis=-1)


def heads(x, n_heads):
    return x.reshape(x.shape[0], x.shape[1], n_heads, HEAD_DIM)


def split_projection(h, w_in):
    bounds = np.cumsum(IN_SIZES)[:-1].tolist()
    return jnp.split(h @ w_in, bounds, axis=-1)


def rope_tables(n):
    t = jnp.arange(n)
    row = (t // GRID_W).astype(jnp.float32)
    col = (t % GRID_W).astype(jnp.float32)
    n_freq = HEAD_DIM // 4
    inv = ROPE_THETA ** (-jnp.arange(n_freq, dtype=jnp.float32) / n_freq)
    ang = jnp.concatenate([row[:, None] * inv, col[:, None] * inv], axis=-1)
    return jnp.cos(ang), jnp.sin(ang)


def apply_rope(x, cos, sin):
    B, n, H, Dh = x.shape
    xp = x.astype(jnp.float32).reshape(B, n, H, Dh // 2, 2)
    x0, x1 = xp[..., 0], xp[..., 1]
    c = cos[None, :, None, :]
    s = sin[None, :, None, :]
    out = jnp.stack([x0 * c - x1 * s, x0 * s + x1 * c], axis=-1)
    return out.reshape(B, n, H, Dh).astype(x.dtype)


def context_attention(q, k, v, sink):
    B, L, Hq, Dh = q.shape
    Hkv = k.shape[2]
    G = Hq // Hkv
    qg = q.reshape(B, L, Hkv, G, Dh)
    s = jnp.einsum('bqkgd,bskd->bkgqs', qg, k).astype(jnp.float32) * (Dh ** -0.5)
    if sink is not None:
        sk = jnp.broadcast_to(sink.reshape(Hkv, G, 1, 1).astype(jnp.float32), (B, Hkv, G, L, 1))
        s = jnp.concatenate([s, sk], axis=-1)
    p = jax.nn.softmax(s, axis=-1)[..., :L].astype(v.dtype)
    o = jnp.einsum('bkgqs,bskd->bqkgd', p, v)
    return o.reshape(B, L, Hq * Dh)


def neighbourhood_attention_latent(q, k, v, k_ctx, v_ctx, rpb):
    B, n, H, Dh = q.shape
    L = k_ctx.shape[1]
    rows = n // GRID_W
    kr = min(NA_MAX_ROWS, rows)
    ncb = GRID_W // NA_COLS
    scale = Dh ** -0.5
    r = jnp.arange(rows)
    row_start = jnp.clip(r - kr // 2, 0, rows - kr)
    row_idx = row_start[:, None] + jnp.arange(kr)[None, :]
    cb = jnp.arange(ncb)
    col_start = jnp.clip(cb * NA_COLS - NA_COLS // 2, 0, GRID_W - NA_KEY_COLS)
    col_idx = col_start[:, None] + jnp.arange(NA_KEY_COLS)[None, :]
    kg = k.reshape(B, rows, GRID_W, H, Dh)
    vg = v.reshape(B, rows, GRID_W, H, Dh)
    ri = row_idx[:, None, :, None]
    ci = col_idx[None, :, None, :]
    kb = kg[:, ri, ci]
    vb = vg[:, ri, ci]
    qb = q.reshape(B, rows, ncb, NA_COLS, H, Dh)
    s_loc = jnp.einsum('brjqhd,brjkwhd->bhrjqkw', qb, kb).astype(jnp.float32) * scale
    qcol = cb[:, None] * NA_COLS + jnp.arange(NA_COLS)[None, :]
    win_start = jnp.clip(qcol - NA_COLS // 2, 0, GRID_W - NA_COLS)
    kcol = col_idx[:, None, :]
    valid = (kcol >= win_start[..., None]) & (kcol < win_start[..., None] + NA_COLS)
    dr_i = row_idx - r[:, None] + NA_MAX_ROWS - 1
    dc_i = jnp.clip(kcol - qcol[..., None] + NA_COLS - 1, 0, 2 * NA_COLS - 2)
    bias = rpb[:, dr_i[:, None, None, :, None], dc_i[None, :, :, None, :]]
    s_loc = s_loc + bias[None].astype(jnp.float32)
    s_loc = jnp.where(valid[:, :, None, :], s_loc, NEG_INF)
    s_loc = s_loc.reshape(B, H, rows, ncb, NA_COLS, kr * NA_KEY_COLS)
    s_ctx = jnp.einsum('bnhd,blhd->bhnl', q, k_ctx).astype(jnp.float32) * scale
    s_ctx = s_ctx.reshape(B, H, rows, ncb, NA_COLS, L)
    p = jax.nn.softmax(jnp.concatenate([s_loc, s_ctx], axis=-1), axis=-1).astype(v.dtype)
    p_loc = p[..., :kr * NA_KEY_COLS].reshape(B, H, rows, ncb, NA_COLS, kr, NA_KEY_COLS)
    p_ctx = p[..., kr * NA_KEY_COLS:]
    o = jnp.einsum('bhrjqkw,brjkwhd->brjqhd', p_loc, vb) + jnp.einsum('bhrjql,blhd->brjqhd', p_ctx, v_ctx)
    return o.reshape(B, n, H * Dh)


def window_attention_latent(q_rot, q_plain, k, v, k_ctx, v_ctx, sink):
    B, n, Hq, Dh = q_rot.shape
    Hkv = k.shape[2]
    G = Hq // Hkv
    L = k_ctx.shape[1]
    bs = WIN_BLOCK
    nb = n // bs
    scale = Dh ** -0.5
    pad = jnp.zeros((B, bs, Hkv, Dh), k.dtype)
    kp = jnp.concatenate([pad, k, pad], axis=1).reshape(B, nb + 2, bs, Hkv, Dh)
    vp = jnp.concatenate([pad.astype(v.dtype), v, pad.astype(v.dtype)], axis=1).reshape(B, nb + 2, bs, Hkv, Dh)
    kband = jnp.concatenate([kp[:, :-2], kp[:, 1:-1], kp[:, 2:]], axis=2)
    vband = jnp.concatenate([vp[:, :-2], vp[:, 1:-1], vp[:, 2:]], axis=2)
    qb = q_rot.reshape(B, nb, bs, Hkv, G, Dh)
    s_loc = jnp.einsum('bjqkgd,bjskd->bkgjqs', qb, kband).astype(jnp.float32) * scale
    blk = jnp.arange(nb)[:, None, None]
    qpos = blk * bs + jnp.arange(bs)[None, :, None]
    kpos = (blk - 1) * bs + jnp.arange(3 * bs)[None, None, :]
    valid = (jnp.abs(kpos - qpos) <= WIN_RADIUS) & (kpos >= 0) & (kpos < n)
    s_loc = jnp.where(valid, s_loc, NEG_INF)
    qc = q_plain.reshape(B, nb, bs, Hkv, G, Dh)
    s_ctx = jnp.einsum('bjqkgd,blkd->bkgjql', qc, k_ctx).astype(jnp.float32) * scale
    s_sink = jnp.broadcast_to(sink.reshape(1, Hkv, G, 1, 1, 1).astype(jnp.float32), (B, Hkv, G, nb, bs, 1))
    p = jax.nn.softmax(jnp.concatenate([s_loc, s_ctx, s_sink], axis=-1), axis=-1).astype(v.dtype)
    o = (jnp.einsum('bkgjqs,bjskd->bjqkgd', p[..., :3 * bs], vband)
         + jnp.einsum('bkgjql,blkd->bjqkgd', p[..., 3 * bs:3 * bs + L], v_ctx))
    return o.reshape(B, n, Hq * Dh)


def pool_mix(u, w_pool, scale):
    B, n, _ = u.shape
    ug = u.reshape(B, n, POOL_GROUPS, POOL_GROUP_W).astype(jnp.float32)
    cs = jnp.concatenate([jnp.zeros((B, 1, POOL_GROUPS, POOL_GROUP_W), jnp.float32), jnp.cumsum(ug, axis=1)], axis=1)
    half = jnp.array([w // 2 for w in POOL_WINDOWS], jnp.int32)
    t = jnp.arange(n)[:, None]
    lo = jnp.clip(t - half[None, :], 0, n)
    hi = jnp.clip(t + half[None, :], 0, n)
    gidx = jnp.arange(POOL_GROUPS)[None, :]
    mean = (cs[:, hi, gidx] - cs[:, lo, gidx]) / (hi - lo).astype(jnp.float32)[None, :, :, None]
    pooled = (mean - ug).astype(u.dtype)
    y = jnp.einsum('bngc,gcd->bngd', pooled, w_pool).reshape(B, n, POOL_GROUPS * POOL_GROUP_W)
    return y * scale


def fourier_mix(u):
    B, n, _ = u.shape
    ug = u.reshape(B, n, FNET_GROUPS, FNET_GROUP_W).astype(jnp.float32)
    y = jnp.fft.fft2(ug, axes=(1, 3), norm='ortho').real
    return y.reshape(B, n, FNET_GROUPS * FNET_GROUP_W).astype(u.dtype)


def merge_branches(branches, gate_logits, w_branch, w_out):
    B, n, _ = gate_logits.shape
    y = jnp.stack(branches, axis=2)
    proj = jnp.einsum('bniw,iwd->bnid', y, w_branch)
    gates = jax.nn.sigmoid(gate_logits.reshape(B, n, N_BRANCH, -1).astype(jnp.float32)).astype(proj.dtype)
    return jnp.sum(gates * proj, axis=2) @ w_out


def context_mixer(h, p):
    na_q, na_k, na_v, pool_u, win_q, win_k, win_v, fnet_u, gate_logits = split_projection(h, p['w_in'])
    ka, va = heads(na_k, NA_HEADS), heads(na_v, NA_HEADS)
    kw, vw = heads(win_k, WIN_KV_HEADS), heads(win_v, WIN_KV_HEADS)
    y_na = context_attention(heads(na_q, NA_HEADS), ka, va, None)
    y_pool = pool_mix(pool_u, p['pool_w'], p['pool_scale'])
    y_win = context_attention(heads(win_q, WIN_HEADS), kw, vw, p['win_sink'])
    y_fnet = fourier_mix(fnet_u)
    mix = merge_branches([y_na, y_pool, y_win, y_fnet], gate_logits, p['w_branch'], p['w_out'])
    return mix, ka, va, kw, vw


def latent_mixer(h, p, ck_na, cv_na, ck_win, cv_win, cos, sin):
    na_q, na_k, na_v, pool_u, win_q, win_k, win_v, fnet_u, gate_logits = split_projection(h, p['w_in'])
    y_na = neighbourhood_attention_latent(heads(na_q, NA_HEADS), heads(na_k, NA_HEADS), heads(na_v, NA_HEADS),
                                          ck_na, cv_na, p['na_rpb'])
    y_pool = pool_mix(pool_u, p['pool_w'], p['pool_scale'])
    qw = heads(win_q, WIN_HEADS)
    y_win = window_attention_latent(apply_rope(qw, cos, sin), qw, apply_rope(heads(win_k, WIN_KV_HEADS), cos, sin),
                                    heads(win_v, WIN_KV_HEADS), ck_win, cv_win, p['win_sink'])
    y_fnet = fourier_mix(fnet_u)
    return merge_branches([y_na, y_pool, y_win, y_fnet], gate_logits, p['w_branch'], p['w_out'])


def expert_choice_ffn(h, w_router, w_gate, w_up, w_down):
    B, n, D = h.shape
    cap = EC_CAPACITY_FACTOR * n // N_EXPERTS
    aff = jax.nn.softmax((h @ w_router).astype(jnp.float32), axis=-1)
    vals, idx = lax.top_k(jnp.swapaxes(aff, 1, 2), cap)
    bi = jnp.arange(B)[:, None, None]
    xg = h[bi, idx]
    a = jnp.einsum('becd,edf->becf', xg, w_gate)
    b = jnp.einsum('becd,edf->becf', xg, w_up)
    out = jnp.einsum('becf,efd->becd', jax.nn.silu(a) * b, w_down) * vals[..., None].astype(h.dtype)
    return jnp.zeros_like(h).at[bi, idx].add(out)


def channel_sublayer(x, shift, scale, gate, p):
    h = rmsnorm(x, p['gain'][2]) * (1 + scale) + shift
    ff = expert_choice_ffn(h, p['w_router'], p['w_e_gate'], p['w_e_up'], p['w_e_down'])
    return x + gate * rmsnorm(ff, p['gain'][3])


def setup_inputs(seed: int = 0) -> dict:
    key = jax.random.key(seed)
    ks = jax.random.split(key, 24)

    def nrm(k, shape, s):
        return jax.random.normal(k, shape, jnp.float32) * s

    return {
        'x_prompt': nrm(ks[0], (BATCH, SEQ, D_MODEL), 1.0),
        'x_sample': nrm(ks[1], (DEC_BATCH, DEC_SEQ, D_MODEL), 1.0),
        'c': nrm(ks[2], (DEC_BATCH, D_MODEL), 1.0),
        'cache_na_k': nrm(ks[3], (DEC_BATCH, DEPTH, PAST_LEN, NA_HEADS, HEAD_DIM), 1.0),
        'cache_na_v': nrm(ks[4], (DEC_BATCH, DEPTH, PAST_LEN, NA_HEADS, HEAD_DIM), 1.0),
        'cache_win_k': nrm(ks[5], (DEC_BATCH, DEPTH, PAST_LEN, WIN_KV_HEADS, HEAD_DIM), 1.0),
        'cache_win_v': nrm(ks[6], (DEC_BATCH, DEPTH, PAST_LEN, WIN_KV_HEADS, HEAD_DIM), 1.0),
        'c_ctx': nrm(ks[7], (D_MODEL,), 1.0),
        'w_ada': nrm(ks[8], (DEPTH, D_MODEL, 6 * D_MODEL), 0.5 * D_MODEL ** -0.5),
        'b_ada': nrm(ks[9], (DEPTH, 6 * D_MODEL), 0.02),
        'norm_gain': 1.0 + nrm(ks[10], (DEPTH, 4, D_MODEL), 0.05),
        'w_in': nrm(ks[11], (DEPTH, D_MODEL, IN_COLS), D_MODEL ** -0.5),
        'na_rpb': nrm(ks[12], (DEPTH, NA_HEADS, 2 * NA_MAX_ROWS - 1, 2 * NA_COLS - 1), 0.1),
        'win_sink': nrm(ks[13], (DEPTH, WIN_HEADS), 0.5),
        'pool_w': nrm(ks[14], (DEPTH, POOL_GROUPS, POOL_GROUP_W, POOL_GROUP_W), POOL_GROUP_W ** -0.5),
        'pool_scale': 1.0 + nrm(ks[15], (DEPTH, POOL_GROUPS * POOL_GROUP_W), 0.1),
        'w_branch': nrm(ks[16], (DEPTH, N_BRANCH, BRANCH_W, D_MODEL), BRANCH_W ** -0.5),
        'w_out': nrm(ks[17], (DEPTH, D_MODEL, D_MODEL), D_MODEL ** -0.5),
        'w_router': nrm(ks[18], (DEPTH, D_MODEL, N_EXPERTS), D_MODEL ** -0.5),
        'w_e_gate': nrm(ks[19], (DEPTH, N_EXPERTS, D_MODEL, EXPERT_FF), D_MODEL ** -0.5),
        'w_e_up': nrm(ks[20], (DEPTH, N_EXPERTS, D_MODEL, EXPERT_FF), D_MODEL ** -0.5),
        'w_e_down': nrm(ks[21], (DEPTH, N_EXPERTS, EXPERT_FF, D_MODEL), EXPERT_FF ** -0.5),
    }


def reference(x_prompt, x_sample, c, cache_na_k, cache_na_v, cache_win_k, cache_win_v,
              c_ctx, w_ada, b_ada, norm_gain, w_in, na_rpb, win_sink, pool_w, pool_scale,
              w_branch, w_out, w_router, w_e_gate, w_e_up, w_e_down):
    cos, sin = rope_tables(x_sample.shape[1])
    xp = x_prompt
    xs = x_sample
    na_k_list, na_v_list, win_k_list, win_v_list = [], [], [], []
    for l in range(DEPTH):
        p = {'w_in': w_in[l], 'na_rpb': na_rpb[l], 'win_sink': win_sink[l], 'pool_w': pool_w[l],
             'pool_scale': pool_scale[l], 'w_branch': w_branch[l], 'w_out': w_out[l],
             'w_router': w_router[l], 'w_e_gate': w_e_gate[l], 'w_e_up': w_e_up[l],
             'w_e_down': w_e_down[l], 'gain': norm_gain[l]}
        sh1, sc1, g1, sh2, sc2, g2 = modulation(c_ctx[None, :], w_ada[l], b_ada[l])
        h = rmsnorm(xp, p['gain'][0]) * (1 + sc1) + sh1
        mix, ka, va, kw, vw = context_mixer(h, p)
        xp = xp + g1 * rmsnorm(mix, p['gain'][1])
        xp = channel_sublayer(xp, sh2, sc2, g2, p)
        na_k_list.append(ka)
        na_v_list.append(va)
        win_k_list.append(kw)
        win_v_list.append(vw)
        sh1, sc1, g1, sh2, sc2, g2 = modulation(c, w_ada[l], b_ada[l])
        h = rmsnorm(xs, p['gain'][0]) * (1 + sc1) + sh1
        mix = latent_mixer(h, p, cache_na_k[:, l], cache_na_v[:, l], cache_win_k[:, l], cache_win_v[:, l], cos, sin)
        xs = xs + g1 * rmsnorm(mix, p['gain'][1])
        xs = channel_sublayer(xs, sh2, sc2, g2, p)
    new_na_k = jnp.stack(na_k_list, axis=1)
    new_na_v = jnp.stack(na_v_list, axis=1)
    new_win_k = jnp.stack(win_k_list, axis=1)
    new_win_v = jnp.stack(win_v_list, axis=1)
    return (xp, xs, new_na_k, new_na_v, new_win_k, new_win_v)
```

```python
import functools

import numpy as np
import jax
import jax.numpy as jnp
from jax import lax
from jax.experimental import pallas as pl
from jax.experimental.pallas import tpu as pltpu

F32 = jnp.float32
BF16 = jnp.bfloat16
I32 = jnp.int32

D_MODEL = 2048
GRID_W = 64
HEAD_DIM = 128
N_BRANCH = 4
BRANCH_W = 512
NA_HEADS = 4
NA_MAX_ROWS = 8
NA_COLS = 16
WIN_HEADS = 4
WIN_KV_HEADS = 2
WIN_RADIUS = 128
WIN_BLOCK = 128
POOL_GROUPS = 4
POOL_GROUP_W = 128
POOL_WINDOWS = (2, 4, 8, 16)
FNET_GROUPS = 4
FNET_GROUP_W = 128
N_EXPERTS = 16
EC_CAPACITY_FACTOR = 2
ROPE_THETA = 10000.0
NORM_EPS = 1e-6
NEG_INF = -1e30
ATTN_SCALE = HEAD_DIM ** -0.5

SMALL_COLS = 3584
COL_NA_Q, COL_NA_K, COL_NA_V = 0, 4, 8
COL_POOL = 12
COL_WIN_Q, COL_WIN_K, COL_WIN_V = 16, 20, 22
COL_FNET = 24
GATE_COLS = N_BRANCH * D_MODEL

VMEM_LIMIT_BYTES = 56 * 1024 * 1024
LANES = 128
SUBLANES = 8
BF16_ROWS = 16

H_SLAB_W = LANES
H_SLABS = D_MODEL // H_SLAB_W


def _params(*sem):
    return pltpu.CompilerParams(dimension_semantics=sem, vmem_limit_bytes=VMEM_LIMIT_BYTES)


def _dot(a, b):
    return jnp.dot(a.astype(BF16), b.astype(BF16), preferred_element_type=F32)


def _dot_nt(a, b):
    return lax.dot_general(a.astype(BF16), b.astype(BF16), (((1,), (1,)), ((), ())),
                           preferred_element_type=F32)


def _rms(x, gain):
    return x * lax.rsqrt(jnp.mean(x * x, axis=-1, keepdims=True) + NORM_EPS) * gain


def _mod_kernel(c_ref, w_ref, b_ref, o_ref):
    c = c_ref[...]
    s = c * jax.nn.sigmoid(c)
    o_ref[...] = jnp.dot(s, w_ref[...], preferred_element_type=F32,
                         precision=lax.Precision.HIGHEST) + b_ref[...]


def _modulation(cvecs, w_ada, b_ada):
    depth, d, n6 = w_ada.shape
    tn = 1024
    return pl.pallas_call(
        _mod_kernel,
        grid=(depth, n6 // tn),
        in_specs=[pl.BlockSpec((SUBLANES, d), lambda l, j: (0, 0)),
                  pl.BlockSpec((None, d, tn), lambda l, j: (l, 0, j)),
                  pl.BlockSpec((None, 1, tn), lambda l, j: (l, 0, j))],
        out_specs=pl.BlockSpec((None, SUBLANES, tn), lambda l, j: (l, 0, j)),
        out_shape=jax.ShapeDtypeStruct((depth, SUBLANES, n6), F32),
        compiler_params=_params("arbitrary", "arbitrary"),
        name="modulation",
    )(cvecs, w_ada, b_ada.reshape(depth, 1, n6))


def _inproj_kernel(x_ref, g_ref, mod_ref, w_ref, ps_ref, gl_ref, h_scr, *, n_small):
    j = pl.program_id(1)

    @pl.when(j == 0)
    def _():
        d = D_MODEL
        h = _rms(x_ref[...], g_ref[...]) * (1.0 + mod_ref[:, d:2 * d]) + mod_ref[:, 0:d]
        h_scr[...] = h.astype(BF16)

    r = jnp.dot(h_scr[...], w_ref[...].astype(BF16), preferred_element_type=F32)

    @pl.when(j < n_small)
    def _():
        ps_ref[...] = r

    @pl.when(j >= n_small)
    def _():
        gl_ref[...] = r.astype(BF16)


def _in_projection(x, gain, mod_rows, w_in):
    n, d = x.shape
    tm, tn = 1024, 512
    n_small = SMALL_COLS // tn
    n_tiles = w_in.shape[1] // tn
    rows_per_mod = n // mod_rows.shape[0]
    return pl.pallas_call(
        functools.partial(_inproj_kernel, n_small=n_small),
        grid=(n // tm, n_tiles),
        in_specs=[pl.BlockSpec((tm, d), lambda i, j: (i, 0)),
                  pl.BlockSpec((1, d), lambda i, j: (0, 0)),
                  pl.BlockSpec((None, 1, 6 * d), lambda i, j: ((i * tm) // rows_per_mod, 0, 0)),
                  pl.BlockSpec((d, tn), lambda i, j: (0, j))],
        out_specs=[pl.BlockSpec((tm, tn), lambda i, j: (i, jnp.minimum(j, n_small - 1))),
                   pl.BlockSpec((tm, tn), lambda i, j: (i, jnp.maximum(j - n_small, 0)))],
        out_shape=[jax.ShapeDtypeStruct((n, SMALL_COLS), F32),
                   jax.ShapeDtypeStruct((n, GATE_COLS), BF16)],
        scratch_shapes=[pltpu.VMEM((tm, d), BF16)],
        compiler_params=_params("arbitrary", "arbitrary"),
        name="in_projection",
    )(x, gain.reshape(1, d), mod_rows, w_in)


def _ctx_attn_kernel(sink_ref, q_ref, k_ref, v_ref, o_ref):
    sk = sink_ref[pl.program_id(1)]
    s = _dot_nt(q_ref[...], k_ref[...]) * ATTN_SCALE
    m = jnp.maximum(jnp.max(s, axis=-1, keepdims=True), sk)
    p = jnp.exp(s - m)
    den = jnp.sum(p, axis=-1, keepdims=True) + jnp.exp(sk - m)
    o_ref[...] = (_dot(p, v_ref[...]) / den).astype(o_ref.dtype)


def _ctx_attention(ps, sinks8, batch, seq):
    n_heads = NA_HEADS + WIN_HEADS
    group = WIN_HEADS // WIN_KV_HEADS

    def qcol(h):
        return jnp.where(h < NA_HEADS, COL_NA_Q + h, COL_WIN_Q + h - NA_HEADS)

    def kcol(h):
        return jnp.where(h < NA_HEADS, COL_NA_K + h, COL_WIN_K + (h - NA_HEADS) // group)

    def vcol(h):
        return jnp.where(h < NA_HEADS, COL_NA_V + h, COL_WIN_V + (h - NA_HEADS) // group)

    blk = (seq, HEAD_DIM)
    return pl.pallas_call(
        _ctx_attn_kernel,
        grid=(batch, n_heads),
        in_specs=[pl.BlockSpec(memory_space=pltpu.SMEM),
                  pl.BlockSpec(blk, lambda b, h: (b, qcol(h))),
                  pl.BlockSpec(blk, lambda b, h: (b, kcol(h))),
                  pl.BlockSpec(blk, lambda b, h: (b, vcol(h)))],
        out_specs=pl.BlockSpec(blk, lambda b, h: (b, h)),
        out_shape=jax.ShapeDtypeStruct((batch * seq, n_heads * HEAD_DIM), BF16),
        compiler_params=_params("arbitrary", "arbitrary"),
        name="ctx_attention",
    )(sinks8, ps, ps, ps)


NA_QROWS = 4
NA_QBLK = NA_QROWS * GRID_W
NA_KBLK = 3 * NA_QBLK


def _na_bias_table(rpb, rows):
    n_blocks = rows // NA_QROWS
    jrep = np.array([0, 1, n_blocks - 1])
    rr = np.arange(NA_QROWS)
    qc = np.arange(GRID_W)
    kk = np.arange(3 * NA_QROWS)
    kc = np.arange(GRID_W)
    r = NA_QROWS * jrep[:, None] + rr[None, :]
    krow = NA_QROWS * (jrep[:, None] - 1) + kk[None, :]
    row_start = np.clip(r - NA_MAX_ROWS // 2, 0, rows - NA_MAX_ROWS)
    kr3 = krow[:, None, :]
    vrow = ((kr3 >= row_start[:, :, None]) & (kr3 < row_start[:, :, None] + NA_MAX_ROWS)
            & (kr3 >= 0) & (kr3 < rows))
    dr = np.clip(kr3 - r[:, :, None] + NA_MAX_ROWS - 1, 0, 2 * NA_MAX_ROWS - 2)
    ws = np.clip(qc - NA_COLS // 2, 0, GRID_W - NA_COLS)
    vcol = (kc[None, :] >= ws[:, None]) & (kc[None, :] < ws[:, None] + NA_COLS)
    dc = np.clip(kc[None, :] - qc[:, None] + NA_COLS - 1, 0, 2 * NA_COLS - 2)
    bias = rpb[:, dr[:, :, None, :, None], dc[None, None, :, None, :]]
    valid = vrow[:, :, None, :, None] & vcol[None, None, :, None, :]
    tbl = jnp.where(valid[None], bias.astype(F32), NEG_INF)
    return tbl.reshape(rpb.shape[0], 3, NA_QBLK, NA_KBLK)


def _na_kernel(q_ref, kp_ref, kc_ref, kn_ref, vp_ref, vc_ref, vn_ref, kx_ref, vx_ref, t_ref, o_ref):
    q = q_ref[...].astype(BF16)
    k = jnp.concatenate([kp_ref[...], kc_ref[...], kn_ref[...]], axis=0)
    v = jnp.concatenate([vp_ref[...], vc_ref[...], vn_ref[...]], axis=0)
    s_loc = _dot_nt(q, k) * ATTN_SCALE + t_ref[...]
    s_ctx = _dot_nt(q, kx_ref[...]) * ATTN_SCALE
    m = jnp.maximum(jnp.max(s_loc, axis=-1, keepdims=True), jnp.max(s_ctx, axis=-1, keepdims=True))
    p_loc = jnp.exp(s_loc - m)
    p_ctx = jnp.exp(s_ctx - m)
    den = jnp.sum(p_loc, axis=-1, keepdims=True) + jnp.sum(p_ctx, axis=-1, keepdims=True)
    o = _dot(p_loc, v) + _dot(p_ctx, vx_ref[...])
    o_ref[...] = (o / den).astype(o_ref.dtype)


def _latent_na(ps, cache_k, cache_v, table, batch, n):
    nb = n // NA_QBLK
    past = cache_k.shape[1]
    blk = (NA_QBLK, HEAD_DIM)

    def cur(col):
        return pl.BlockSpec(blk, lambda b, h, j: (b * nb + j, col + h))

    def prev(col):
        return pl.BlockSpec(blk, lambda b, h, j: (b * nb + jnp.maximum(j - 1, 0), col + h))

    def nxt(col):
        return pl.BlockSpec(blk, lambda b, h, j: (b * nb + jnp.minimum(j + 1, nb - 1), col + h))

    cache_spec = pl.BlockSpec((None, past, HEAD_DIM), lambda b, h, j: (b, 0, h))
    table_spec = pl.BlockSpec(
        (None, None, NA_QBLK, NA_KBLK),
        lambda b, h, j: (h, jnp.where(j == 0, 0, jnp.where(j == nb - 1, 2, 1)), 0, 0))
    return pl.pallas_call(
        _na_kernel,
        grid=(batch, NA_HEADS, nb),
        in_specs=[cur(COL_NA_Q), prev(COL_NA_K), cur(COL_NA_K), nxt(COL_NA_K),
                  prev(COL_NA_V), cur(COL_NA_V), nxt(COL_NA_V),
                  cache_spec, cache_spec, table_spec],
        out_specs=pl.BlockSpec(blk, lambda b, h, j: (b * nb + j, h)),
        out_shape=jax.ShapeDtypeStruct((batch * n, NA_HEADS * HEAD_DIM), BF16),
        compiler_params=_params("arbitrary", "arbitrary", "arbitrary"),
        name="latent_na",
    )(ps, ps, ps, ps, ps, ps, ps, cache_k, cache_v, table)


def _rope_tables(n):
    t = jnp.arange(n)
    row = (t // GRID_W).astype(F32)
    col = (t % GRID_W).astype(F32)
    n_freq = HEAD_DIM // 4
    inv = ROPE_THETA ** (-jnp.arange(n_freq, dtype=F32) / n_freq)
    ang = jnp.concatenate([row[:, None] * inv, col[:, None] * inv], axis=-1)
    cos = jnp.repeat(jnp.cos(ang), 2, axis=-1)
    sin = jnp.repeat(jnp.sin(ang), 2, axis=-1)
    sign = jnp.tile(jnp.array([-1.0, 1.0], F32), HEAD_DIM // 2)
    return cos, sin * sign


def _rope(x, cos, sin_signed):
    lane = lax.broadcasted_iota(I32, x.shape, 1)
    partner = jnp.where(lane % 2 == 0, pltpu.roll(x, HEAD_DIM - 1, 1), pltpu.roll(x, 1, 1))
    return x * cos + partner * sin_signed


def _win_kernel(sink_ref, q_ref, kp_ref, kc_ref, kn_ref, vp_ref, vc_ref, vn_ref, kx_ref, vx_ref,
                cq_ref, sq_ref, cp_ref, sp_ref, cc_ref, sc_ref, cn_ref, sn_ref, o_ref, *, nb):
    kv = pl.program_id(1)
    j = pl.program_id(2)
    bs = WIN_BLOCK
    group = WIN_HEADS // WIN_KV_HEADS
    qb = q_ref[...]
    q = jnp.concatenate([qb[:, g * HEAD_DIM:(g + 1) * HEAD_DIM] for g in range(group)], axis=0)
    cq = jnp.concatenate([cq_ref[...]] * group, axis=0)
    sq = jnp.concatenate([sq_ref[...]] * group, axis=0)
    q_rot = _rope(q, cq, sq)
    k = jnp.concatenate([_rope(kp_ref[...], cp_ref[...], sp_ref[...]),
                         _rope(kc_ref[...], cc_ref[...], sc_ref[...]),
                         _rope(kn_ref[...], cn_ref[...], sn_ref[...])], axis=0)
    v = jnp.concatenate([vp_ref[...], vc_ref[...], vn_ref[...]], axis=0)
    s_loc = _dot_nt(q_rot, k) * ATTN_SCALE
    qq = lax.broadcasted_iota(I32, s_loc.shape, 0) % bs
    kk = lax.broadcasted_iota(I32, s_loc.shape, 1)
    valid = jnp.abs(kk - bs - qq) <= WIN_RADIUS
    valid = valid & ((kk >= bs) | (j > 0)) & ((kk < 2 * bs) | (j < nb - 1))
    s_loc = jnp.where(valid, s_loc, NEG_INF)
    s_ctx = _dot_nt(q, kx_ref[...]) * ATTN_SCALE
    rows = lax.broadcasted_iota(I32, (group * bs, 1), 0)
    sk = jnp.zeros((group * bs, 1), F32)
    for g in range(group):
        sk = jnp.where(rows // bs == g, sink_ref[kv * group + g], sk)
    m = jnp.maximum(jnp.maximum(jnp.max(s_loc, axis=-1, keepdims=True),
                                jnp.max(s_ctx, axis=-1, keepdims=True)), sk)
    p_loc = jnp.exp(s_loc - m)
    p_ctx = jnp.exp(s_ctx - m)
    den = (jnp.sum(p_loc, axis=-1, keepdims=True) + jnp.sum(p_ctx, axis=-1, keepdims=True)
           + jnp.exp(sk - m))
    o = (_dot(p_loc, v) + _dot(p_ctx, vx_ref[...])) / den
    o_ref[...] = jnp.concatenate([o[g * bs:(g + 1) * bs] for g in range(group)],
                                 axis=1).astype(o_ref.dtype)


def _latent_win(ps, cache_k, cache_v, sinks, cos, sin, batch, n):
    nb = n // WIN_BLOCK
    past = cache_k.shape[1]
    group = WIN_HEADS // WIN_KV_HEADS
    blk = (WIN_BLOCK, HEAD_DIM)

    def cur_idx(j):
        return j

    def prev_idx(j):
        return jnp.maximum(j - 1, 0)

    def next_idx(j):
        return jnp.minimum(j + 1, nb - 1)

    def tok(col, fn):
        return pl.BlockSpec(blk, lambda b, kv, j: (b * nb + fn(j), col + kv))

    def rope_spec(fn):
        return pl.BlockSpec(blk, lambda b, kv, j: (fn(j), 0))

    q_spec = pl.BlockSpec((WIN_BLOCK, group * HEAD_DIM),
                          lambda b, kv, j: (b * nb + j, COL_WIN_Q // group + kv))
    cache_spec = pl.BlockSpec((None, past, HEAD_DIM), lambda b, kv, j: (b, 0, kv))
    return pl.pallas_call(
        functools.partial(_win_kernel, nb=nb),
        grid=(batch, WIN_KV_HEADS, nb),
        in_specs=[pl.BlockSpec(memory_space=pltpu.SMEM), q_spec,
                  tok(COL_WIN_K, prev_idx), tok(COL_WIN_K, cur_idx), tok(COL_WIN_K, next_idx),
                  tok(COL_WIN_V, prev_idx), tok(COL_WIN_V, cur_idx), tok(COL_WIN_V, next_idx),
                  cache_spec, cache_spec,
                  rope_spec(cur_idx), rope_spec(cur_idx), rope_spec(prev_idx), rope_spec(prev_idx),
                  rope_spec(cur_idx), rope_spec(cur_idx), rope_spec(next_idx), rope_spec(next_idx)],
        out_specs=pl.BlockSpec((WIN_BLOCK, group * HEAD_DIM), lambda b, kv, j: (b * nb + j, kv)),
        out_shape=jax.ShapeDtypeStruct((batch * n, WIN_HEADS * HEAD_DIM), BF16),
        compiler_params=_params("arbitrary", "arbitrary", "arbitrary"),
        name="latent_win",
    )(sinks, ps, ps, ps, ps, ps, ps, ps, cache_k, cache_v,
      cos, sin, cos, sin, cos, sin, cos, sin)


POOL_TILE = 256
POOL_HALO = SUBLANES


def _pool_kernel(up_ref, uc_ref, un_ref, w_ref, sc_ref, o_ref, ext, *, tiles_per_seq, n):
    lt = pl.program_id(0) % tiles_per_seq
    tm = POOL_TILE
    gw = POOL_GROUP_W
    ext[0:POOL_HALO, :] = jnp.where(lt > 0, up_ref[...], 0.0)
    ext[POOL_HALO:POOL_HALO + tm, :] = uc_ref[...]
    ext[POOL_HALO + tm:2 * POOL_HALO + tm, :] = jnp.where(lt < tiles_per_seq - 1, un_ref[...], 0.0)
    pos = lt * tm + lax.broadcasted_iota(I32, (tm, 1), 0)
    for g in range(POOL_GROUPS):
        half = POOL_WINDOWS[g] // 2
        lanes = slice(g * gw, (g + 1) * gw)
        acc = ext[pl.ds(POOL_HALO - half, tm), lanes]
        for dlt in range(-half + 1, half):
            acc = acc + ext[pl.ds(POOL_HALO + dlt, tm), lanes]
        cnt = (jnp.minimum(pos + half, n) - jnp.maximum(pos - half, 0)).astype(F32)
        pooled = acc / cnt - uc_ref[:, lanes]
        y = _dot(pooled, w_ref[g]) * sc_ref[:, lanes]
        o_ref[:, lanes] = y.astype(o_ref.dtype)


def _pool_mix(ps, pool_w, pool_scale, n_seq, n):
    tm = POOL_TILE
    tps = n // tm
    n_tiles = n_seq * tps
    halo_per_tile = tm // POOL_HALO
    width = POOL_GROUPS * POOL_GROUP_W
    colb = COL_POOL * LANES // width
    last_halo = n_tiles * halo_per_tile - 1
    return pl.pallas_call(
        functools.partial(_pool_kernel, tiles_per_seq=tps, n=n),
        grid=(n_tiles,),
        in_specs=[pl.BlockSpec((POOL_HALO, width),
                               lambda t: (jnp.maximum(t * halo_per_tile - 1, 0), colb)),
                  pl.BlockSpec((tm, width), lambda t: (t, colb)),
                  pl.BlockSpec((POOL_HALO, width),
                               lambda t: (jnp.minimum((t + 1) * halo_per_tile, last_halo), colb)),
                  pl.BlockSpec((POOL_GROUPS, POOL_GROUP_W, POOL_GROUP_W), lambda t: (0, 0, 0)),
                  pl.BlockSpec((1, width), lambda t: (0, 0))],
        out_specs=pl.BlockSpec((tm, width), lambda t: (t, 0)),
        out_shape=jax.ShapeDtypeStruct((n_seq * n, width), BF16),
        scratch_shapes=[pltpu.VMEM((tm + 2 * POOL_HALO, width), F32)],
        compiler_params=_params("arbitrary"),
        name="pool_mix",
    )(ps, ps, ps, pool_w, pool_scale.reshape(1, width))


def _dft_tables(n):
    k = jnp.arange(n, dtype=I32)
    if n <= 1024:
        ang = ((k[:, None] * k[None, :]) % n).astype(F32) * (2.0 * np.pi / n)
        return jnp.cos(ang).astype(BF16), jnp.sin(ang).astype(BF16)
    lo = 64
    hi = n // lo
    t1 = jnp.arange(hi, dtype=I32)
    t0 = jnp.arange(lo, dtype=I32)
    a = ((k[:, None] * t1[None, :] * lo) % n).astype(F32) * (2.0 * np.pi / n)
    b = ((k[:, None] * t0[None, :]) % n).astype(F32) * (2.0 * np.pi / n)
    ca, sa, cb, sb = jnp.cos(a), jnp.sin(a), jnp.cos(b), jnp.sin(b)
    cos = ca[:, :, None] * cb[:, None, :] - sa[:, :, None] * sb[:, None, :]
    sin = sa[:, :, None] * cb[:, None, :] + ca[:, :, None] * sb[:, None, :]
    return cos.reshape(n, n).astype(BF16), sin.reshape(n, n).astype(BF16)


def _fnet_kernel(u_ref, cn_ref, sn_ref, cc_ref, sc_ref, o_ref, v1, v2, *, scale):
    @pl.when(pl.program_id(1) == 0)
    def _():
        gw = FNET_GROUP_W
        for g in range(FNET_GROUPS):
            lanes = slice(g * gw, (g + 1) * gw)
            ug = u_ref[:, lanes].astype(BF16)
            v1[:, lanes] = jnp.dot(ug, cc_ref[...], preferred_element_type=F32).astype(BF16)
            v2[:, lanes] = jnp.dot(ug, sc_ref[...], preferred_element_type=F32).astype(BF16)

    y = (jnp.dot(cn_ref[...], v1[...], preferred_element_type=F32)
         - jnp.dot(sn_ref[...], v2[...], preferred_element_type=F32))
    o_ref[...] = (y * scale).astype(o_ref.dtype)


def _fourier_mix(ps, cos_n, sin_n, cos_c, sin_c, n_seq, n):
    width = FNET_GROUPS * FNET_GROUP_W
    colb = COL_FNET * LANES // width
    tm = min(n, 512)
    nt = n // tm
    scale = float(1.0 / np.sqrt(n * FNET_GROUP_W))
    return pl.pallas_call(
        functools.partial(_fnet_kernel, scale=scale),
        grid=(n_seq, nt),
        in_specs=[pl.BlockSpec((n, width), lambda b, i: (b, colb)),
                  pl.BlockSpec((tm, n), lambda b, i: (i, 0)),
                  pl.BlockSpec((tm, n), lambda b, i: (i, 0)),
                  pl.BlockSpec((FNET_GROUP_W, FNET_GROUP_W), lambda b, i: (0, 0)),
                  pl.BlockSpec((FNET_GROUP_W, FNET_GROUP_W), lambda b, i: (0, 0))],
        out_specs=pl.BlockSpec((tm, width), lambda b, i: (b * nt + i, 0)),
        out_shape=jax.ShapeDtypeStruct((n_seq * n, width), BF16),
        scratch_shapes=[pltpu.VMEM((n, width), BF16), pltpu.VMEM((n, width), BF16)],
        compiler_params=_params("arbitrary", "arbitrary"),
        name="fourier_mix",
    )(ps, cos_n, sin_n, cos_c, sin_c)


MERGE_TILE = 256


def _merge_kernel(x_ref, y0_ref, y1_ref, y2_ref, y3_ref, gl_ref, wb_ref, wo_ref, gain_ref, mod_ref,
                  wr_ref, x1_ref, h_ref, aff_ref):
    d = D_MODEL
    tm = x_ref.shape[0]
    acc = None
    for i, y_ref in enumerate((y0_ref, y1_ref, y2_ref, y3_ref)):
        proj = jnp.dot(y_ref[...], wb_ref[i], preferred_element_type=F32)
        gate = jax.nn.sigmoid(gl_ref[:, i * d:(i + 1) * d].astype(F32))
        acc = gate * proj if acc is None else acc + gate * proj
    mix = jnp.dot(acc.astype(BF16), wo_ref[...], preferred_element_type=F32)
    x1 = x_ref[...] + mod_ref[:, 2 * d:3 * d] * _rms(mix, gain_ref[1:2, :])
    x1_ref[...] = x1
    h = _rms(x1, gain_ref[2:3, :]) * (1.0 + mod_ref[:, 4 * d:5 * d]) + mod_ref[:, 3 * d:4 * d]
    for s in range(H_SLABS):
        h_ref[pl.ds(s, tm, stride=H_SLABS), :] = h[:, s * H_SLAB_W:(s + 1) * H_SLAB_W]
    logits = lax.dot_general(wr_ref[...], h, (((1,), (1,)), ((), ())),
                             preferred_element_type=F32, precision=lax.Precision.HIGHEST)
    logits = logits - jnp.max(logits, axis=0, keepdims=True)
    e = jnp.exp(logits)
    aff_ref[...] = e / jnp.sum(e, axis=0, keepdims=True)


def _merge(x, ys, gl, wb, wo, gains, mod_rows, wr_t):
    n, d = x.shape
    tm = MERGE_TILE
    rows_per_mod = n // mod_rows.shape[0]
    const = dict(pipeline_mode=pl.Buffered(1))
    y_specs = [pl.BlockSpec((tm, BRANCH_W), functools.partial(lambda i, c: (i, c), c=cb))
               for _, cb in ys]
    return pl.pallas_call(
        _merge_kernel,
        grid=(n // tm,),
        in_specs=[pl.BlockSpec((tm, d), lambda i: (i, 0))] + y_specs + [
            pl.BlockSpec((tm, GATE_COLS), lambda i: (i, 0)),
            pl.BlockSpec((N_BRANCH, BRANCH_W, d), lambda i: (0, 0, 0), **const),
            pl.BlockSpec((d, d), lambda i: (0, 0), **const),
            pl.BlockSpec((4, d), lambda i: (0, 0)),
            pl.BlockSpec((None, 1, 6 * d), lambda i: ((i * tm) // rows_per_mod, 0, 0)),
            pl.BlockSpec((N_EXPERTS, d), lambda i: (0, 0))],
        out_specs=[pl.BlockSpec((tm, d), lambda i: (i, 0)),
                   pl.BlockSpec((tm * H_SLABS, H_SLAB_W), lambda i: (i, 0)),
                   pl.BlockSpec((N_EXPERTS, tm), lambda i: (0, i))],
        out_shape=[jax.ShapeDtypeStruct((n, d), F32),
                   jax.ShapeDtypeStruct((n * H_SLABS, H_SLAB_W), F32),
                   jax.ShapeDtypeStruct((N_EXPERTS, n), F32)],
        compiler_params=_params("arbitrary"),
        name="merge",
    )(x, *[a for a, _ in ys], gl, wb, wo, gains, mod_rows, wr_t)


CUMSUM_CHUNK = 256


def _lane_cumsum(x, upper):
    n = x.shape[1]
    carry = jnp.zeros((x.shape[0], 1), F32)
    out = []
    for c in range(n // CUMSUM_CHUNK):
        xc = x[:, c * CUMSUM_CHUNK:(c + 1) * CUMSUM_CHUNK]
        out.append(jnp.dot(xc.astype(BF16), upper, preferred_element_type=F32) + carry)
        carry = carry + jnp.sum(xc, axis=1, keepdims=True)
    return jnp.concatenate(out, axis=1) if len(out) > 1 else out[0]


def _route_kernel(a_ref, posm_ref, idx_ref, *, n, cap):
    bits = lax.bitcast_convert_type(a_ref[...], I32)
    thr = jnp.zeros((N_EXPERTS, 1), I32)
    for bit in range(30, -1, -1):
        cand = thr | (1 << bit)
        cnt = jnp.sum((bits >= cand).astype(F32), axis=1, keepdims=True)
        thr = jnp.where(cnt >= cap, cand, thr)
    gt = bits > thr
    eq = bits == thr
    need = cap - jnp.sum(gt.astype(F32), axis=1, keepdims=True)
    r = lax.broadcasted_iota(I32, (CUMSUM_CHUNK, CUMSUM_CHUNK), 0)
    c = lax.broadcasted_iota(I32, (CUMSUM_CHUNK, CUMSUM_CHUNK), 1)
    upper = (r <= c).astype(BF16)
    cum_eq = _lane_cumsum(eq.astype(F32), upper)
    sel = gt | (eq & (cum_eq <= need))
    pos = _lane_cumsum(sel.astype(F32), upper) - 1.0
    posm = jnp.where(sel, pos, -1.0).astype(I32)
    posm_ref[...] = posm
    tok = lax.broadcasted_iota(I32, (n, LANES), 0)
    lane = lax.broadcasted_iota(I32, (n, LANES), 1)
    tmat = jnp.where(lane == 0, tok // 64, jnp.where(lane == 1, tok % 64, 0)).astype(BF16)
    slot = lax.broadcasted_iota(I32, (cap, n), 0)
    for e in range(N_EXPERTS):
        onehot = (posm[e:e + 1, :] == slot).astype(BF16)
        idx_ref[e] = jnp.dot(onehot, tmat, preferred_element_type=F32)


def _route(aff_t, n_seq, n):
    cap = EC_CAPACITY_FACTOR * n // N_EXPERTS
    return pl.pallas_call(
        functools.partial(_route_kernel, n=n, cap=cap),
        grid=(n_seq,),
        in_specs=[pl.BlockSpec((N_EXPERTS, n), lambda b: (0, b))],
        out_specs=[pl.BlockSpec((N_EXPERTS, n), lambda b: (0, b)),
                   pl.BlockSpec((None, N_EXPERTS, cap, LANES), lambda b: (b, 0, 0, 0))],
        out_shape=[jax.ShapeDtypeStruct((N_EXPERTS, n_seq * n), I32),
                   jax.ShapeDtypeStruct((n_seq, N_EXPERTS, cap, LANES), F32)],
        compiler_params=_params("arbitrary"),
        name="route",
    )(aff_t)


COMBINE_TILE = 128
COMBINE_WIN = COMBINE_TILE + BF16_ROWS


def _expert_kernel(idx_ref, h_ref, wg_ref, wu_ref, wd_ref, o_ref, xg, xb, acc, sem, *, m):
    e = pl.program_id(0)
    f = pl.program_id(1)

    @pl.when(f == 0)
    def _():
        def issue(p, carry):
            src = pl.multiple_of(idx_ref[e * m + p] * H_SLABS, H_SLABS)
            dst = pl.multiple_of(p * H_SLABS, H_SLABS)
            pltpu.make_async_copy(h_ref.at[pl.ds(src, H_SLABS), :],
                                  xg.at[pl.ds(dst, H_SLABS), :], sem).start()
            return carry

        lax.fori_loop(0, m, issue, 0, unroll=8)
        pltpu.make_async_copy(h_ref.at[pl.ds(0, m * H_SLABS), :], xg, sem).wait()
        for s in range(H_SLABS):
            xb[:, s * H_SLAB_W:(s + 1) * H_SLAB_W] = xg[pl.ds(s, m, stride=H_SLABS), :].astype(BF16)

    x = xb[...]
    a = jnp.dot(x, wg_ref[...].astype(BF16), preferred_element_type=F32)
    b = jnp.dot(x, wu_ref[...].astype(BF16), preferred_element_type=F32)
    hmid = (a * jax.nn.sigmoid(a) * b).astype(BF16)
    contrib = jnp.dot(hmid, wd_ref[...].astype(BF16), preferred_element_type=F32)

    @pl.when(f == 0)
    def _():
        acc[...] = contrib

    @pl.when(f > 0)
    def _():
        acc[...] += contrib

    @pl.when(f == pl.num_programs(1) - 1)
    def _():
        o_ref[0:m, :] = acc[...].astype(o_ref.dtype)
        o_ref[m:, :] = jnp.zeros((o_ref.shape[0] - m, o_ref.shape[1]), o_ref.dtype)


def _expert_ffn(idx_flat, h_slabs, w_gate, w_up, w_down, m, tf):
    n_exp, d, ff = w_gate.shape
    m_pad = m + COMBINE_WIN
    grid_spec = pltpu.PrefetchScalarGridSpec(
        num_scalar_prefetch=1,
        grid=(n_exp, ff // tf),
        in_specs=[pl.BlockSpec(memory_space=pl.ANY),
                  pl.BlockSpec((None, d, tf), lambda e, f, idx: (e, 0, f)),
                  pl.BlockSpec((None, d, tf), lambda e, f, idx: (e, 0, f)),
                  pl.BlockSpec((None, tf, d), lambda e, f, idx: (e, f, 0))],
        out_specs=pl.BlockSpec((None, m_pad, d), lambda e, f, idx: (e, 0, 0)),
        scratch_shapes=[pltpu.VMEM((m * H_SLABS, H_SLAB_W), F32),
                        pltpu.VMEM((m, d), BF16),
                        pltpu.VMEM((m, d), F32),
                        pltpu.SemaphoreType.DMA(())])
    return pl.pallas_call(
        functools.partial(_expert_kernel, m=m),
        grid_spec=grid_spec,
        out_shape=jax.ShapeDtypeStruct((n_exp, m_pad, d), BF16),
        compiler_params=_params("arbitrary", "arbitrary"),
        name="expert_ffn",
    )(idx_flat, h_slabs, w_gate, w_up, w_down)


def _combine_kernel(w0_ref, x_ref, posm_ref, aff_ref, gain_ref, mod_ref, *rest):
    win_refs = rest[:N_EXPERTS]
    o_ref = rest[N_EXPERTS]
    t = pl.program_id(0)
    d = D_MODEL
    tt = COMBINE_TILE
    pad = jnp.zeros((LANES - 2 * N_EXPERTS, tt), F32)
    both = jnp.concatenate([posm_ref[...].astype(F32), aff_ref[...], pad], axis=0).T
    slot = lax.broadcasted_iota(I32, (tt, COMBINE_WIN), 1).astype(F32)
    ff = jnp.zeros((tt, d), F32)
    for e in range(N_EXPERTS):
        rel = both[:, e:e + 1] - w0_ref[t * N_EXPERTS + e].astype(F32)
        onehot = (rel == slot).astype(BF16)
        c = jnp.dot(onehot, win_refs[e][...], preferred_element_type=F32)
        ff = ff + both[:, N_EXPERTS + e:N_EXPERTS + e + 1] * c
    o_ref[...] = x_ref[...] + mod_ref[:, 5 * d:6 * d] * _rms(ff, gain_ref[3:4, :])


def _combine(w0_flat, x1, posm, aff_t, gains, mod_rows, ys):
    n, d = x1.shape
    tt = COMBINE_TILE
    rows_per_mod = n // mod_rows.shape[0]

    def win_spec(e):
        return pl.BlockSpec(
            (None, pl.Element(COMBINE_WIN), pl.Element(d)),
            lambda t, w0: (e, pl.multiple_of(w0[t * N_EXPERTS + e], BF16_ROWS), 0))

    grid_spec = pltpu.PrefetchScalarGridSpec(
        num_scalar_prefetch=1,
        grid=(n // tt,),
        in_specs=[pl.BlockSpec((tt, d), lambda t, w0: (t, 0)),
                  pl.BlockSpec((N_EXPERTS, tt), lambda t, w0: (0, t)),
                  pl.BlockSpec((N_EXPERTS, tt), lambda t, w0: (0, t)),
                  pl.BlockSpec((4, d), lambda t, w0: (0, 0)),
                  pl.BlockSpec((None, 1, 6 * d), lambda t, w0: ((t * tt) // rows_per_mod, 0, 0))]
                 + [win_spec(e) for e in range(N_EXPERTS)],
        out_specs=pl.BlockSpec((tt, d), lambda t, w0: (t, 0)))
    return pl.pallas_call(
        _combine_kernel,
        grid_spec=grid_spec,
        out_shape=jax.ShapeDtypeStruct((n, d), F32),
        compiler_params=_params("arbitrary"),
        name="combine",
    )(w0_flat, x1, posm, aff_t, gains, mod_rows, *([ys] * N_EXPERTS))


def _slot_indices(idx_parts, n_seq, n):
    tok = (idx_parts[..., 0] * 64.0 + idx_parts[..., 1]).astype(I32)
    tok = tok + (jnp.arange(n_seq, dtype=I32) * n)[:, None, None]
    return jnp.transpose(tok, (1, 0, 2)).reshape(-1)


def _window_starts(posm, n_seq, n):
    cap = EC_CAPACITY_FACTOR * n // N_EXPERTS
    tt = COMBINE_TILE
    sel = (posm >= 0).reshape(N_EXPERTS, n_seq, n // tt, tt)
    cnt = jnp.sum(sel.astype(I32), axis=-1)
    before = jnp.cumsum(cnt, axis=-1) - cnt
    start = before + (jnp.arange(n_seq, dtype=I32) * cap)[None, :, None]
    start = (start // BF16_ROWS) * BF16_ROWS
    return jnp.transpose(start.reshape(N_EXPERTS, -1), (1, 0)).reshape(-1)


def _global_positions(posm, n_seq, n):
    cap = EC_CAPACITY_FACTOR * n // N_EXPERTS
    base = jnp.repeat(jnp.arange(n_seq, dtype=I32) * cap, n)[None, :]
    return jnp.where(posm >= 0, posm + base, -1)


def _channel_sublayer(x1, h_slabs, aff_t, mod_rows, gains, w_gate, w_up, w_down, n_seq, n, tf):
    cap = EC_CAPACITY_FACTOR * n // N_EXPERTS
    posm, idx_parts = _route(aff_t, n_seq, n)
    idx_flat = _slot_indices(idx_parts, n_seq, n)
    ys = _expert_ffn(idx_flat, h_slabs, w_gate, w_up, w_down, n_seq * cap, tf)
    w0 = _window_starts(posm, n_seq, n)
    return _combine(w0, x1, _global_positions(posm, n_seq, n), aff_t, gains, mod_rows, ys)


def kernel(x_prompt, x_sample, c, cache_na_k, cache_na_v, cache_win_k, cache_win_v, c_ctx, w_ada, b_ada, norm_gain, w_in, na_rpb, win_sink, pool_w, pool_scale, w_branch, w_out, w_router, w_e_gate, w_e_up, w_e_down):
    batch, seq, d = x_prompt.shape
    dec_batch, dec_seq, _ = x_sample.shape
    depth = w_in.shape[0]
    past = cache_na_k.shape[2]
    assert d == D_MODEL and dec_seq % GRID_W == 0 and dec_batch <= SUBLANES - 1

    cvecs = jnp.zeros((SUBLANES, d), F32).at[0].set(c_ctx).at[1:1 + dec_batch].set(c)
    mods = _modulation(cvecs, w_ada, b_ada)

    cos_r, sin_r = _rope_tables(dec_seq)
    cos_ctx, sin_ctx = _dft_tables(seq)
    cos_lat, sin_lat = _dft_tables(dec_seq)
    cos_c, sin_c = _dft_tables(FNET_GROUP_W)

    xp = x_prompt.reshape(batch * seq, d)
    xs = x_sample.reshape(dec_batch * dec_seq, d)
    na_k, na_v, win_k, win_v = [], [], [], []
    no_sink = jnp.full((NA_HEADS,), NEG_INF, F32)
    for l in range(depth):
        gains = norm_gain[l]
        wb = w_branch[l].astype(BF16)
        wo = w_out[l].astype(BF16)
        wr_t = w_router[l].T
        mod_ctx = mods[l, 0:1].reshape(1, 1, 6 * d)
        mod_lat = mods[l, 1:1 + dec_batch].reshape(dec_batch, 1, 6 * d)

        ps, gl = _in_projection(xp, gains[0], mod_ctx, w_in[l])
        y_attn = _ctx_attention(ps, jnp.concatenate([no_sink, win_sink[l]]), batch, seq)
        y_pool = _pool_mix(ps, pool_w[l], pool_scale[l], batch, seq)
        y_fnet = _fourier_mix(ps, cos_ctx, sin_ctx, cos_c, sin_c, batch, seq)
        x1, h_slabs, aff_t = _merge(xp, [(y_attn, 0), (y_pool, 0), (y_attn, 1), (y_fnet, 0)],
                                    gl, wb, wo, gains, mod_ctx, wr_t)
        xp = _channel_sublayer(x1, h_slabs, aff_t, mod_ctx, gains, w_e_gate[l], w_e_up[l],
                               w_e_down[l], batch, seq, 512)
        hd = HEAD_DIM
        na_k.append(ps[:, COL_NA_K * hd:(COL_NA_K + NA_HEADS) * hd])
        na_v.append(ps[:, COL_NA_V * hd:(COL_NA_V + NA_HEADS) * hd])
        win_k.append(ps[:, COL_WIN_K * hd:(COL_WIN_K + WIN_KV_HEADS) * hd])
        win_v.append(ps[:, COL_WIN_V * hd:(COL_WIN_V + WIN_KV_HEADS) * hd])

        ps, gl = _in_projection(xs, gains[0], mod_lat, w_in[l])
        table = _na_bias_table(na_rpb[l], dec_seq // GRID_W)
        y_na = _latent_na(ps, cache_na_k[:, l].reshape(dec_batch, past, NA_HEADS * hd),
                          cache_na_v[:, l].reshape(dec_batch, past, NA_HEADS * hd),
                          table, dec_batch, dec_seq)
        y_win = _latent_win(ps, cache_win_k[:, l].reshape(dec_batch, past, WIN_KV_HEADS * hd),
                            cache_win_v[:, l].reshape(dec_batch, past, WIN_KV_HEADS * hd),
                            win_sink[l], cos_r, sin_r, dec_batch, dec_seq)
        y_pool = _pool_mix(ps, pool_w[l], pool_scale[l], dec_batch, dec_seq)
        y_fnet = _fourier_mix(ps, cos_lat, sin_lat, cos_c, sin_c, dec_batch, dec_seq)
        x1, h_slabs, aff_t = _merge(xs, [(y_na, 0), (y_pool, 0), (y_win, 0), (y_fnet, 0)],
                                    gl, wb, wo, gains, mod_lat, wr_t)
        xs = _channel_sublayer(x1, h_slabs, aff_t, mod_lat, gains, w_e_gate[l], w_e_up[l],
                               w_e_down[l], dec_batch, dec_seq, 256)

    def stack(parts, heads):
        return jnp.stack([p.reshape(batch, seq, heads, HEAD_DIM) for p in parts], axis=1)

    return (xp.reshape(batch, seq, d), xs.reshape(dec_batch, dec_seq, d),
            stack(na_k, NA_HEADS), stack(na_v, NA_HEADS),
            stack(win_k, WIN_KV_HEADS), stack(win_v, WIN_KV_HEADS))
```

```python
import functools

import numpy as np
import jax
import jax.numpy as jnp
from jax import lax
from jax.experimental import pallas as pl
from jax.experimental.pallas import tpu as pltpu

F32 = jnp.float32
BF16 = jnp.bfloat16
I32 = jnp.int32

D_MODEL = 2048
GRID_W = 64
HEAD_DIM = 128
N_BRANCH = 4
BRANCH_W = 512
NA_HEADS = 4
NA_MAX_ROWS = 8
NA_COLS = 16
WIN_HEADS = 4
WIN_KV_HEADS = 2
WIN_RADIUS = 128
WIN_BLOCK = 128
POOL_GROUPS = 4
POOL_GROUP_W = 128
POOL_WINDOWS = (2, 4, 8, 16)
FNET_GROUPS = 4
FNET_GROUP_W = 128
N_EXPERTS = 16
EC_CAPACITY_FACTOR = 2
ROPE_THETA = 10000.0
NORM_EPS = 1e-6
NEG_INF = -1e30
ATTN_SCALE = HEAD_DIM ** -0.5

SMALL_COLS = 3584
COL_NA_Q, COL_NA_K, COL_NA_V = 0, 4, 8
COL_POOL = 12
COL_WIN_Q, COL_WIN_K, COL_WIN_V = 16, 20, 22
COL_FNET = 24
GATE_COLS = N_BRANCH * D_MODEL

VMEM_LIMIT_BYTES = 56 * 1024 * 1024
LANES = 128
SUBLANES = 8
BF16_ROWS = 16

H_SLAB_W = LANES
H_SLABS = D_MODEL // H_SLAB_W


def _params(*sem):
    return pltpu.CompilerParams(dimension_semantics=sem, vmem_limit_bytes=VMEM_LIMIT_BYTES)


def _dot(a, b):
    return jnp.dot(a.astype(BF16), b.astype(BF16), preferred_element_type=F32)


def _dot_nt(a, b):
    return lax.dot_general(a.astype(BF16), b.astype(BF16), (((1,), (1,)), ((), ())),
                           preferred_element_type=F32)


def _rms(x, gain):
    return x * lax.rsqrt(jnp.mean(x * x, axis=-1, keepdims=True) + NORM_EPS) * gain


def _mod_kernel(c_ref, w_ref, b_ref, o_ref):
    c = c_ref[...]
    s = c * jax.nn.sigmoid(c)
    o_ref[...] = jnp.dot(s, w_ref[...], preferred_element_type=F32,
                         precision=lax.Precision.HIGHEST) + b_ref[...]


def _modulation(cvecs, w_ada, b_ada):
    depth, d, n6 = w_ada.shape
    tn = 1024
    return pl.pallas_call(
        _mod_kernel,
        grid=(depth, n6 // tn),
        in_specs=[pl.BlockSpec((SUBLANES, d), lambda l, j: (0, 0)),
                  pl.BlockSpec((None, d, tn), lambda l, j: (l, 0, j)),
                  pl.BlockSpec((None, 1, tn), lambda l, j: (l, 0, j))],
        out_specs=pl.BlockSpec((None, SUBLANES, tn), lambda l, j: (l, 0, j)),
        out_shape=jax.ShapeDtypeStruct((depth, SUBLANES, n6), F32),
        compiler_params=_params("arbitrary", "arbitrary"),
        name="modulation",
    )(cvecs, w_ada, b_ada.reshape(depth, 1, n6))


def _inproj_kernel(x_ref, g_ref, mod_ref, w_ref, ps_ref, gl_ref, h_scr, *, n_small):
    j = pl.program_id(1)

    @pl.when(j == 0)
    def _():
        d = D_MODEL
        h = _rms(x_ref[...], g_ref[...]) * (1.0 + mod_ref[:, d:2 * d]) + mod_ref[:, 0:d]
        h_scr[...] = h.astype(BF16)

    r = jnp.dot(h_scr[...], w_ref[...].astype(BF16), preferred_element_type=F32)

    @pl.when(j < n_small)
    def _():
        ps_ref[...] = r

    @pl.when(j >= n_small)
    def _():
        gl_ref[...] = r.astype(BF16)


def _in_projection(x, gain, mod_rows, w_in, layer):
    n, d = x.shape
    tm, tn = 1024, 512
    n_small = SMALL_COLS // tn
    n_tiles = w_in.shape[2] // tn
    rows_per_mod = n // mod_rows.shape[0]
    return pl.pallas_call(
        functools.partial(_inproj_kernel, n_small=n_small),
        grid=(n // tm, n_tiles),
        in_specs=[pl.BlockSpec((tm, d), lambda i, j: (i, 0)),
                  pl.BlockSpec((1, d), lambda i, j: (0, 0)),
                  pl.BlockSpec((None, 1, 6 * d), lambda i, j: ((i * tm) // rows_per_mod, 0, 0)),
                  pl.BlockSpec((None, d, tn), lambda i, j: (layer, 0, j))],
        out_specs=[pl.BlockSpec((tm, tn), lambda i, j: (i, jnp.minimum(j, n_small - 1))),
                   pl.BlockSpec((tm, tn), lambda i, j: (i, jnp.maximum(j - n_small, 0)))],
        out_shape=[jax.ShapeDtypeStruct((n, SMALL_COLS), F32),
                   jax.ShapeDtypeStruct((n, GATE_COLS), BF16)],
        scratch_shapes=[pltpu.VMEM((tm, d), BF16)],
        compiler_params=_params("arbitrary", "arbitrary"),
        name="in_projection",
    )(x, gain.reshape(1, d), mod_rows, w_in)


def _ctx_attn_kernel(sink_ref, q_ref, k_ref, v_ref, o_ref):
    sk = sink_ref[pl.program_id(1)]
    s = _dot_nt(q_ref[...], k_ref[...]) * ATTN_SCALE
    m = jnp.maximum(jnp.max(s, axis=-1, keepdims=True), sk)
    p = jnp.exp(s - m)
    den = jnp.sum(p, axis=-1, keepdims=True) + jnp.exp(sk - m)
    o_ref[...] = (_dot(p, v_ref[...]) / den).astype(o_ref.dtype)


def _ctx_attention(ps, sinks8, batch, seq):
    n_heads = NA_HEADS + WIN_HEADS
    group = WIN_HEADS // WIN_KV_HEADS

    def qcol(h):
        return jnp.where(h < NA_HEADS, COL_NA_Q + h, COL_WIN_Q + h - NA_HEADS)

    def kcol(h):
        return jnp.where(h < NA_HEADS, COL_NA_K + h, COL_WIN_K + (h - NA_HEADS) // group)

    def vcol(h):
        return jnp.where(h < NA_HEADS, COL_NA_V + h, COL_WIN_V + (h - NA_HEADS) // group)

    blk = (seq, HEAD_DIM)
    return pl.pallas_call(
        _ctx_attn_kernel,
        grid=(batch, n_heads),
        in_specs=[pl.BlockSpec(memory_space=pltpu.SMEM),
                  pl.BlockSpec(blk, lambda b, h: (b, qcol(h))),
                  pl.BlockSpec(blk, lambda b, h: (b, kcol(h))),
                  pl.BlockSpec(blk, lambda b, h: (b, vcol(h)))],
        out_specs=pl.BlockSpec(blk, lambda b, h: (b, h)),
        out_shape=jax.ShapeDtypeStruct((batch * seq, n_heads * HEAD_DIM), BF16),
        compiler_params=_params("arbitrary", "arbitrary"),
        name="ctx_attention",
    )(sinks8, ps, ps, ps)


NA_QROWS = 4
NA_QBLK = NA_QROWS * GRID_W
NA_KBLK = 3 * NA_QBLK


def _na_bias_table(rpb, rows):
    n_blocks = rows // NA_QROWS
    jrep = np.array([0, 1, n_blocks - 1])
    rr = np.arange(NA_QROWS)
    qc = np.arange(GRID_W)
    kk = np.arange(3 * NA_QROWS)
    kc = np.arange(GRID_W)
    r = NA_QROWS * jrep[:, None] + rr[None, :]
    krow = NA_QROWS * (jrep[:, None] - 1) + kk[None, :]
    row_start = np.clip(r - NA_MAX_ROWS // 2, 0, rows - NA_MAX_ROWS)
    kr3 = krow[:, None, :]
    vrow = ((kr3 >= row_start[:, :, None]) & (kr3 < row_start[:, :, None] + NA_MAX_ROWS)
            & (kr3 >= 0) & (kr3 < rows))
    dr = np.clip(kr3 - r[:, :, None] + NA_MAX_ROWS - 1, 0, 2 * NA_MAX_ROWS - 2)
    ws = np.clip(qc - NA_COLS // 2, 0, GRID_W - NA_COLS)
    vcol = (kc[None, :] >= ws[:, None]) & (kc[None, :] < ws[:, None] + NA_COLS)
    dc = np.clip(kc[None, :] - qc[:, None] + NA_COLS - 1, 0, 2 * NA_COLS - 2)
    n_dr, n_dc = 2 * NA_MAX_ROWS - 1, 2 * NA_COLS - 1
    sel_r = (dr.reshape(-1)[:, None] == np.arange(n_dr)[None, :]).astype(np.float32)
    sel_c = (np.arange(n_dc)[:, None] == dc.reshape(-1)[None, :]).astype(np.float32)
    rows_sel = jnp.einsum("rd,hdc->hrc", sel_r, rpb.astype(F32), precision=lax.Precision.HIGHEST)
    bias = jnp.einsum("hrc,cq->hrq", rows_sel, sel_c, precision=lax.Precision.HIGHEST)
    bias = bias.reshape(rpb.shape[0], 3, NA_QROWS, 3 * NA_QROWS, GRID_W, GRID_W)
    bias = jnp.transpose(bias, (0, 1, 2, 4, 3, 5))
    valid = vrow[:, :, None, :, None] & vcol[None, None, :, None, :]
    tbl = jnp.where(valid[None], bias, NEG_INF)
    return tbl.reshape(rpb.shape[0], 3, NA_QBLK, NA_KBLK)


def _na_kernel(q_ref, kp_ref, kc_ref, kn_ref, vp_ref, vc_ref, vn_ref, kx_ref, vx_ref, t_ref, o_ref):
    q = q_ref[...].astype(BF16)
    k = jnp.concatenate([kp_ref[...], kc_ref[...], kn_ref[...]], axis=0)
    v = jnp.concatenate([vp_ref[...], vc_ref[...], vn_ref[...]], axis=0)
    s_loc = _dot_nt(q, k) * ATTN_SCALE + t_ref[...]
    s_ctx = _dot_nt(q, kx_ref[...]) * ATTN_SCALE
    m = jnp.maximum(jnp.max(s_loc, axis=-1, keepdims=True), jnp.max(s_ctx, axis=-1, keepdims=True))
    p_loc = jnp.exp(s_loc - m)
    p_ctx = jnp.exp(s_ctx - m)
    den = jnp.sum(p_loc, axis=-1, keepdims=True) + jnp.sum(p_ctx, axis=-1, keepdims=True)
    o = _dot(p_loc, v) + _dot(p_ctx, vx_ref[...])
    o_ref[...] = (o / den).astype(o_ref.dtype)


def _latent_na(ps, cache_k, cache_v, table, batch, n):
    nb = n // NA_QBLK
    past = cache_k.shape[1]
    blk = (NA_QBLK, HEAD_DIM)

    def cur(col):
        return pl.BlockSpec(blk, lambda b, h, j: (b * nb + j, col + h))

    def prev(col):
        return pl.BlockSpec(blk, lambda b, h, j: (b * nb + jnp.maximum(j - 1, 0), col + h))

    def nxt(col):
        return pl.BlockSpec(blk, lambda b, h, j: (b * nb + jnp.minimum(j + 1, nb - 1), col + h))

    cache_spec = pl.BlockSpec((None, past, HEAD_DIM), lambda b, h, j: (b, 0, h))
    table_spec = pl.BlockSpec(
        (None, None, NA_QBLK, NA_KBLK),
        lambda b, h, j: (h, jnp.where(j == 0, 0, jnp.where(j == nb - 1, 2, 1)), 0, 0))
    return pl.pallas_call(
        _na_kernel,
        grid=(batch, NA_HEADS, nb),
        in_specs=[cur(COL_NA_Q), prev(COL_NA_K), cur(COL_NA_K), nxt(COL_NA_K),
                  prev(COL_NA_V), cur(COL_NA_V), nxt(COL_NA_V),
                  cache_spec, cache_spec, table_spec],
        out_specs=pl.BlockSpec(blk, lambda b, h, j: (b * nb + j, h)),
        out_shape=jax.ShapeDtypeStruct((batch * n, NA_HEADS * HEAD_DIM), BF16),
        compiler_params=_params("arbitrary", "arbitrary", "arbitrary"),
        name="latent_na",
    )(ps, ps, ps, ps, ps, ps, ps, cache_k, cache_v, table)


def _rope_tables(n):
    t = jnp.arange(n)
    row = (t // GRID_W).astype(F32)
    col = (t % GRID_W).astype(F32)
    n_freq = HEAD_DIM // 4
    inv = ROPE_THETA ** (-jnp.arange(n_freq, dtype=F32) / n_freq)
    ang = jnp.concatenate([row[:, None] * inv, col[:, None] * inv], axis=-1)
    cos = jnp.repeat(jnp.cos(ang), 2, axis=-1)
    sin = jnp.repeat(jnp.sin(ang), 2, axis=-1)
    sign = jnp.tile(jnp.array([-1.0, 1.0], F32), HEAD_DIM // 2)
    return cos, sin * sign


def _rope(x, cos, sin_signed):
    lane = lax.broadcasted_iota(I32, x.shape, 1)
    partner = jnp.where(lane % 2 == 0, pltpu.roll(x, HEAD_DIM - 1, 1), pltpu.roll(x, 1, 1))
    return x * cos + partner * sin_signed


def _win_kernel(sink_ref, q_ref, kp_ref, kc_ref, kn_ref, vp_ref, vc_ref, vn_ref, kx_ref, vx_ref,
                cq_ref, sq_ref, cp_ref, sp_ref, cc_ref, sc_ref, cn_ref, sn_ref, o_ref, *, nb):
    kv = pl.program_id(1)
    j = pl.program_id(2)
    bs = WIN_BLOCK
    group = WIN_HEADS // WIN_KV_HEADS
    qb = q_ref[...]
    q = jnp.concatenate([qb[:, g * HEAD_DIM:(g + 1) * HEAD_DIM] for g in range(group)], axis=0)
    cq = jnp.concatenate([cq_ref[...]] * group, axis=0)
    sq = jnp.concatenate([sq_ref[...]] * group, axis=0)
    q_rot = _rope(q, cq, sq)
    k = jnp.concatenate([_rope(kp_ref[...], cp_ref[...], sp_ref[...]),
                         _rope(kc_ref[...], cc_ref[...], sc_ref[...]),
                         _rope(kn_ref[...], cn_ref[...], sn_ref[...])], axis=0)
    v = jnp.concatenate([vp_ref[...], vc_ref[...], vn_ref[...]], axis=0)
    s_loc = _dot_nt(q_rot, k) * ATTN_SCALE
    qq = lax.broadcasted_iota(I32, s_loc.shape, 0) % bs
    kk = lax.broadcasted_iota(I32, s_loc.shape, 1)
    valid = jnp.abs(kk - bs - qq) <= WIN_RADIUS
    valid = valid & ((kk >= bs) | (j > 0)) & ((kk < 2 * bs) | (j < nb - 1))
    s_loc = jnp.where(valid, s_loc, NEG_INF)
    s_ctx = _dot_nt(q, kx_ref[...]) * ATTN_SCALE
    rows = lax.broadcasted_iota(I32, (group * bs, 1), 0)
    sk = jnp.zeros((group * bs, 1), F32)
    for g in range(group):
        sk = jnp.where(rows // bs == g, sink_ref[kv * group + g], sk)
    m = jnp.maximum(jnp.maximum(jnp.max(s_loc, axis=-1, keepdims=True),
                                jnp.max(s_ctx, axis=-1, keepdims=True)), sk)
    p_loc = jnp.exp(s_loc - m)
    p_ctx = jnp.exp(s_ctx - m)
    den = (jnp.sum(p_loc, axis=-1, keepdims=True) + jnp.sum(p_ctx, axis=-1, keepdims=True)
           + jnp.exp(sk - m))
    o = (_dot(p_loc, v) + _dot(p_ctx, vx_ref[...])) / den
    o_ref[...] = jnp.concatenate([o[g * bs:(g + 1) * bs] for g in range(group)],
                                 axis=1).astype(o_ref.dtype)


def _latent_win(ps, cache_k, cache_v, sinks, cos, sin, batch, n):
    nb = n // WIN_BLOCK
    past = cache_k.shape[1]
    group = WIN_HEADS // WIN_KV_HEADS
    blk = (WIN_BLOCK, HEAD_DIM)

    def cur_idx(j):
        return j

    def prev_idx(j):
        return jnp.maximum(j - 1, 0)

    def next_idx(j):
        return jnp.minimum(j + 1, nb - 1)

    def tok(col, fn):
        return pl.BlockSpec(blk, lambda b, kv, j: (b * nb + fn(j), col + kv))

    def rope_spec(fn):
        return pl.BlockSpec(blk, lambda b, kv, j: (fn(j), 0))

    q_spec = pl.BlockSpec((WIN_BLOCK, group * HEAD_DIM),
                          lambda b, kv, j: (b * nb + j, COL_WIN_Q // group + kv))
    cache_spec = pl.BlockSpec((None, past, HEAD_DIM), lambda b, kv, j: (b, 0, kv))
    return pl.pallas_call(
        functools.partial(_win_kernel, nb=nb),
        grid=(batch, WIN_KV_HEADS, nb),
        in_specs=[pl.BlockSpec(memory_space=pltpu.SMEM), q_spec,
                  tok(COL_WIN_K, prev_idx), tok(COL_WIN_K, cur_idx), tok(COL_WIN_K, next_idx),
                  tok(COL_WIN_V, prev_idx), tok(COL_WIN_V, cur_idx), tok(COL_WIN_V, next_idx),
                  cache_spec, cache_spec,
                  rope_spec(cur_idx), rope_spec(cur_idx), rope_spec(prev_idx), rope_spec(prev_idx),
                  rope_spec(cur_idx), rope_spec(cur_idx), rope_spec(next_idx), rope_spec(next_idx)],
        out_specs=pl.BlockSpec((WIN_BLOCK, group * HEAD_DIM), lambda b, kv, j: (b * nb + j, kv)),
        out_shape=jax.ShapeDtypeStruct((batch * n, WIN_HEADS * HEAD_DIM), BF16),
        compiler_params=_params("arbitrary", "arbitrary", "arbitrary"),
        name="latent_win",
    )(sinks, ps, ps, ps, ps, ps, ps, ps, cache_k, cache_v,
      cos, sin, cos, sin, cos, sin, cos, sin)


POOL_TILE = 256
POOL_HALO = SUBLANES


def _pool_kernel(up_ref, uc_ref, un_ref, w_ref, sc_ref, o_ref, ext, *, tiles_per_seq, n):
    lt = pl.program_id(0) % tiles_per_seq
    tm = POOL_TILE
    gw = POOL_GROUP_W
    ext[0:POOL_HALO, :] = jnp.where(lt > 0, up_ref[...], 0.0)
    ext[POOL_HALO:POOL_HALO + tm, :] = uc_ref[...]
    ext[POOL_HALO + tm:2 * POOL_HALO + tm, :] = jnp.where(lt < tiles_per_seq - 1, un_ref[...], 0.0)
    pos = lt * tm + lax.broadcasted_iota(I32, (tm, 1), 0)
    for g in range(POOL_GROUPS):
        half = POOL_WINDOWS[g] // 2
        lanes = slice(g * gw, (g + 1) * gw)
        acc = ext[pl.ds(POOL_HALO - half, tm), lanes]
        for dlt in range(-half + 1, half):
            acc = acc + ext[pl.ds(POOL_HALO + dlt, tm), lanes]
        cnt = (jnp.minimum(pos + half, n) - jnp.maximum(pos - half, 0)).astype(F32)
        pooled = acc / cnt - uc_ref[:, lanes]
        y = _dot(pooled, w_ref[g]) * sc_ref[:, lanes]
        o_ref[:, lanes] = y.astype(o_ref.dtype)


def _pool_mix(ps, pool_w, pool_scale, n_seq, n):
    tm = POOL_TILE
    tps = n // tm
    n_tiles = n_seq * tps
    halo_per_tile = tm // POOL_HALO
    width = POOL_GROUPS * POOL_GROUP_W
    colb = COL_POOL * LANES // width
    last_halo = n_tiles * halo_per_tile - 1
    return pl.pallas_call(
        functools.partial(_pool_kernel, tiles_per_seq=tps, n=n),
        grid=(n_tiles,),
        in_specs=[pl.BlockSpec((POOL_HALO, width),
                               lambda t: (jnp.maximum(t * halo_per_tile - 1, 0), colb)),
                  pl.BlockSpec((tm, width), lambda t: (t, colb)),
                  pl.BlockSpec((POOL_HALO, width),
                               lambda t: (jnp.minimum((t + 1) * halo_per_tile, last_halo), colb)),
                  pl.BlockSpec((POOL_GROUPS, POOL_GROUP_W, POOL_GROUP_W), lambda t: (0, 0, 0)),
                  pl.BlockSpec((1, width), lambda t: (0, 0))],
        out_specs=pl.BlockSpec((tm, width), lambda t: (t, 0)),
        out_shape=jax.ShapeDtypeStruct((n_seq * n, width), BF16),
        scratch_shapes=[pltpu.VMEM((tm + 2 * POOL_HALO, width), F32)],
        compiler_params=_params("arbitrary"),
        name="pool_mix",
    )(ps, ps, ps, pool_w, pool_scale.reshape(1, width))


def _dft_tables(n):
    k = jnp.arange(n, dtype=I32)
    if n <= 1024:
        ang = ((k[:, None] * k[None, :]) % n).astype(F32) * (2.0 * np.pi / n)
        return jnp.cos(ang).astype(BF16), jnp.sin(ang).astype(BF16)
    lo = 64
    hi = n // lo
    t1 = jnp.arange(hi, dtype=I32)
    t0 = jnp.arange(lo, dtype=I32)
    a = ((k[:, None] * t1[None, :] * lo) % n).astype(F32) * (2.0 * np.pi / n)
    b = ((k[:, None] * t0[None, :]) % n).astype(F32) * (2.0 * np.pi / n)
    ca, sa, cb, sb = jnp.cos(a), jnp.sin(a), jnp.cos(b), jnp.sin(b)
    cos = ca[:, :, None] * cb[:, None, :] - sa[:, :, None] * sb[:, None, :]
    sin = sa[:, :, None] * cb[:, None, :] + ca[:, :, None] * sb[:, None, :]
    return cos.reshape(n, n).astype(BF16), sin.reshape(n, n).astype(BF16)


def _fnet_kernel(u_ref, cn_ref, sn_ref, cc_ref, sc_ref, o_ref, v1, v2, *, scale):
    @pl.when(pl.program_id(1) == 0)
    def _():
        gw = FNET_GROUP_W
        for g in range(FNET_GROUPS):
            lanes = slice(g * gw, (g + 1) * gw)
            ug = u_ref[:, lanes].astype(BF16)
            v1[:, lanes] = jnp.dot(ug, cc_ref[...], preferred_element_type=F32).astype(BF16)
            v2[:, lanes] = jnp.dot(ug, sc_ref[...], preferred_element_type=F32).astype(BF16)

    y = (jnp.dot(cn_ref[...], v1[...], preferred_element_type=F32)
         - jnp.dot(sn_ref[...], v2[...], preferred_element_type=F32))
    o_ref[...] = (y * scale).astype(o_ref.dtype)


def _fourier_mix(ps, cos_n, sin_n, cos_c, sin_c, n_seq, n):
    width = FNET_GROUPS * FNET_GROUP_W
    colb = COL_FNET * LANES // width
    tm = min(n, 512)
    nt = n // tm
    scale = float(1.0 / np.sqrt(n * FNET_GROUP_W))
    return pl.pallas_call(
        functools.partial(_fnet_kernel, scale=scale),
        grid=(n_seq, nt),
        in_specs=[pl.BlockSpec((n, width), lambda b, i: (b, colb)),
                  pl.BlockSpec((tm, n), lambda b, i: (i, 0)),
                  pl.BlockSpec((tm, n), lambda b, i: (i, 0)),
                  pl.BlockSpec((FNET_GROUP_W, FNET_GROUP_W), lambda b, i: (0, 0)),
                  pl.BlockSpec((FNET_GROUP_W, FNET_GROUP_W), lambda b, i: (0, 0))],
        out_specs=pl.BlockSpec((tm, width), lambda b, i: (b * nt + i, 0)),
        out_shape=jax.ShapeDtypeStruct((n_seq * n, width), BF16),
        scratch_shapes=[pltpu.VMEM((n, width), BF16), pltpu.VMEM((n, width), BF16)],
        compiler_params=_params("arbitrary", "arbitrary"),
        name="fourier_mix",
    )(ps, cos_n, sin_n, cos_c, sin_c)


MERGE_TILE = 256
MERGE_SUBTILES = 2


def _merge_kernel(x_ref, y0_ref, y1_ref, y2_ref, y3_ref, gl_ref, wb_ref, wo_ref, gain_ref, mod_ref,
                  wr_ref, x1_ref, h_ref, aff_ref):
    d = D_MODEL
    ts = x_ref.shape[0] // MERGE_SUBTILES
    for sub in range(MERGE_SUBTILES):
        rows = slice(sub * ts, (sub + 1) * ts)
        acc = None
        for i, y_ref in enumerate((y0_ref, y1_ref, y2_ref, y3_ref)):
            proj = jnp.dot(y_ref[rows, :], wb_ref[i], preferred_element_type=F32)
            gate = jax.nn.sigmoid(gl_ref[rows, i * d:(i + 1) * d].astype(F32))
            acc = gate * proj if acc is None else acc + gate * proj
        mix = jnp.dot(acc.astype(BF16), wo_ref[...], preferred_element_type=F32)
        x1 = x_ref[rows, :] + mod_ref[:, 2 * d:3 * d] * _rms(mix, gain_ref[1:2, :])
        x1_ref[rows, :] = x1
        h = _rms(x1, gain_ref[2:3, :]) * (1.0 + mod_ref[:, 4 * d:5 * d]) + mod_ref[:, 3 * d:4 * d]
        for s in range(H_SLABS):
            h_ref[pl.ds(sub * ts * H_SLABS + s, ts, stride=H_SLABS), :] = (
                h[:, s * H_SLAB_W:(s + 1) * H_SLAB_W])
        logits = lax.dot_general(wr_ref[...], h, (((1,), (1,)), ((), ())),
                                 preferred_element_type=F32, precision=lax.Precision.HIGHEST)
        logits = logits - jnp.max(logits, axis=0, keepdims=True)
        e = jnp.exp(logits)
        aff_ref[:, rows] = e / jnp.sum(e, axis=0, keepdims=True)


def _merge(x, ys, gl, wb, wo, gains, mod_rows, wr_t):
    n, d = x.shape
    tm = MERGE_TILE
    rows_per_mod = n // mod_rows.shape[0]
    const = dict(pipeline_mode=pl.Buffered(1))
    y_specs = [pl.BlockSpec((tm, BRANCH_W), functools.partial(lambda i, c: (i, c), c=cb))
               for _, cb in ys]
    return pl.pallas_call(
        _merge_kernel,
        grid=(n // tm,),
        in_specs=[pl.BlockSpec((tm, d), lambda i: (i, 0))] + y_specs + [
            pl.BlockSpec((tm, GATE_COLS), lambda i: (i, 0)),
            pl.BlockSpec((N_BRANCH, BRANCH_W, d), lambda i: (0, 0, 0), **const),
            pl.BlockSpec((d, d), lambda i: (0, 0), **const),
            pl.BlockSpec((4, d), lambda i: (0, 0)),
            pl.BlockSpec((None, 1, 6 * d), lambda i: ((i * tm) // rows_per_mod, 0, 0)),
            pl.BlockSpec((N_EXPERTS, d), lambda i: (0, 0))],
        out_specs=[pl.BlockSpec((tm, d), lambda i: (i, 0)),
                   pl.BlockSpec((tm * H_SLABS, H_SLAB_W), lambda i: (i, 0)),
                   pl.BlockSpec((N_EXPERTS, tm), lambda i: (0, i))],
        out_shape=[jax.ShapeDtypeStruct((n, d), F32),
                   jax.ShapeDtypeStruct((n * H_SLABS, H_SLAB_W), F32),
                   jax.ShapeDtypeStruct((N_EXPERTS, n), F32)],
        compiler_params=_params("arbitrary"),
        name="merge",
    )(x, *[a for a, _ in ys], gl, wb, wo, gains, mod_rows, wr_t)


CUMSUM_CHUNK = 256


def _lane_cumsum(x, upper):
    n = x.shape[1]
    carry = jnp.zeros((x.shape[0], 1), F32)
    out = []
    for c in range(n // CUMSUM_CHUNK):
        xc = x[:, c * CUMSUM_CHUNK:(c + 1) * CUMSUM_CHUNK]
        out.append(jnp.dot(xc.astype(BF16), upper, preferred_element_type=F32) + carry)
        carry = carry + jnp.sum(xc, axis=1, keepdims=True)
    return jnp.concatenate(out, axis=1) if len(out) > 1 else out[0]


def _route_kernel(a_ref, posm_ref, idx_ref, *, n, cap):
    bits = lax.bitcast_convert_type(a_ref[...], I32)
    thr = jnp.zeros((N_EXPERTS, 1), I32)
    for bit in range(30, -1, -1):
        cand = thr | (1 << bit)
        cnt = jnp.sum((bits >= cand).astype(F32), axis=1, keepdims=True)
        thr = jnp.where(cnt >= cap, cand, thr)
    gt = bits > thr
    eq = bits == thr
    need = cap - jnp.sum(gt.astype(F32), axis=1, keepdims=True)
    r = lax.broadcasted_iota(I32, (CUMSUM_CHUNK, CUMSUM_CHUNK), 0)
    c = lax.broadcasted_iota(I32, (CUMSUM_CHUNK, CUMSUM_CHUNK), 1)
    upper = (r <= c).astype(BF16)
    cum_eq = _lane_cumsum(eq.astype(F32), upper)
    sel = gt | (eq & (cum_eq <= need))
    pos = _lane_cumsum(sel.astype(F32), upper) - 1.0
    posm = jnp.where(sel, pos, -1.0).astype(I32)
    posm_ref[...] = posm
    tok = lax.broadcasted_iota(I32, (n, LANES), 0)
    lane = lax.broadcasted_iota(I32, (n, LANES), 1)
    tmat = jnp.where(lane == 0, tok // 64, jnp.where(lane == 1, tok % 64, 0)).astype(BF16)
    slot = lax.broadcasted_iota(I32, (cap, n), 0)
    for e in range(N_EXPERTS):
        onehot = (posm[e:e + 1, :] == slot).astype(BF16)
        idx_ref[e] = jnp.dot(onehot, tmat, preferred_element_type=F32)


def _route(aff_t, n_seq, n):
    cap = EC_CAPACITY_FACTOR * n // N_EXPERTS
    return pl.pallas_call(
        functools.partial(_route_kernel, n=n, cap=cap),
        grid=(n_seq,),
        in_specs=[pl.BlockSpec((N_EXPERTS, n), lambda b: (0, b))],
        out_specs=[pl.BlockSpec((N_EXPERTS, n), lambda b: (0, b)),
                   pl.BlockSpec((None, N_EXPERTS, cap, LANES), lambda b: (b, 0, 0, 0))],
        out_shape=[jax.ShapeDtypeStruct((N_EXPERTS, n_seq * n), I32),
                   jax.ShapeDtypeStruct((n_seq, N_EXPERTS, cap, LANES), F32)],
        compiler_params=_params("arbitrary"),
        name="route",
    )(aff_t)


COMBINE_TILE = 128
COMBINE_WIN = COMBINE_TILE + BF16_ROWS


def _expert_kernel(idx_ref, h_ref, wg_ref, wu_ref, wd_ref, o_ref, xg, xb, hmid, sem, *, m, nf):
    e = pl.program_id(0)
    f = pl.program_id(1)

    @pl.when(f == 0)
    def _():
        def issue(p, carry):
            src = pl.multiple_of(idx_ref[e * m + p] * H_SLABS, H_SLABS)
            dst = pl.multiple_of(p * H_SLABS, H_SLABS)
            pltpu.make_async_copy(h_ref.at[pl.ds(src, H_SLABS), :],
                                  xg.at[pl.ds(dst, H_SLABS), :], sem).start()
            return carry

        lax.fori_loop(0, m, issue, 0, unroll=8)
        pltpu.make_async_copy(h_ref.at[pl.ds(0, m * H_SLABS), :], xg, sem).wait()
        for s in range(H_SLABS):
            xb[:, s * H_SLAB_W:(s + 1) * H_SLAB_W] = xg[pl.ds(s, m, stride=H_SLABS), :].astype(BF16)

    @pl.when(f < nf)
    def _():
        x = xb[...]
        a = jnp.dot(x, wg_ref[...].astype(BF16), preferred_element_type=F32)
        b = jnp.dot(x, wu_ref[...].astype(BF16), preferred_element_type=F32)
        hmid[f] = (a * jax.nn.sigmoid(a) * b).astype(BF16)

    @pl.when(f >= nf)
    def _():
        tf = hmid.shape[2]
        out = None
        for k in range(nf):
            part = jnp.dot(hmid[k], wd_ref[k * tf:(k + 1) * tf, :].astype(BF16),
                           preferred_element_type=F32)
            out = part if out is None else out + part
        o_ref[0:m, :] = out.astype(o_ref.dtype)
        o_ref[m:, :] = jnp.zeros((o_ref.shape[0] - m, o_ref.shape[1]), o_ref.dtype)


def _expert_ffn(idx_flat, h_slabs, w_gate, w_up, w_down, layer, m, tf, tn):
    _, n_exp, d, ff = w_gate.shape
    m_pad = m + COMBINE_WIN
    nf = ff // tf
    grid_spec = pltpu.PrefetchScalarGridSpec(
        num_scalar_prefetch=1,
        grid=(n_exp, nf + d // tn),
        in_specs=[pl.BlockSpec(memory_space=pl.ANY),
                  pl.BlockSpec((None, None, d, tf),
                               lambda e, f, idx: (layer, e, 0, jnp.minimum(f, nf - 1))),
                  pl.BlockSpec((None, None, d, tf),
                               lambda e, f, idx: (layer, e, 0, jnp.minimum(f, nf - 1))),
                  pl.BlockSpec((None, None, ff, tn),
                               lambda e, f, idx: (layer, e, 0, jnp.maximum(f - nf, 0)))],
        out_specs=pl.BlockSpec((None, m_pad, tn), lambda e, f, idx: (e, 0, jnp.maximum(f - nf, 0))),
        scratch_shapes=[pltpu.VMEM((m * H_SLABS, H_SLAB_W), F32),
                        pltpu.VMEM((m, d), BF16),
                        pltpu.VMEM((nf, m, tf), BF16),
                        pltpu.SemaphoreType.DMA(())])
    return pl.pallas_call(
        functools.partial(_expert_kernel, m=m, nf=nf),
        grid_spec=grid_spec,
        out_shape=jax.ShapeDtypeStruct((n_exp, m_pad, d), BF16),
        compiler_params=_params("arbitrary", "arbitrary"),
        name="expert_ffn",
    )(idx_flat, h_slabs, w_gate, w_up, w_down)


def _combine_kernel(w0_ref, x_ref, posm_ref, aff_ref, gain_ref, mod_ref, *rest):
    win_refs = rest[:N_EXPERTS]
    o_ref = rest[N_EXPERTS]
    t = pl.program_id(0)
    d = D_MODEL
    tt = COMBINE_TILE
    pad = jnp.zeros((LANES - 2 * N_EXPERTS, tt), F32)
    both = jnp.concatenate([posm_ref[...].astype(F32), aff_ref[...], pad], axis=0).T
    slot = lax.broadcasted_iota(I32, (tt, COMBINE_WIN), 1).astype(F32)
    ff = jnp.zeros((tt, d), F32)
    for e in range(N_EXPERTS):
        rel = both[:, e:e + 1] - w0_ref[t * N_EXPERTS + e].astype(F32)
        onehot = (rel == slot).astype(BF16)
        c = jnp.dot(onehot, win_refs[e][...], preferred_element_type=F32)
        ff = ff + both[:, N_EXPERTS + e:N_EXPERTS + e + 1] * c
    o_ref[...] = x_ref[...] + mod_ref[:, 5 * d:6 * d] * _rms(ff, gain_ref[3:4, :])


def _combine(w0_flat, x1, posm, aff_t, gains, mod_rows, ys):
    n, d = x1.shape
    tt = COMBINE_TILE
    rows_per_mod = n // mod_rows.shape[0]

    def win_spec(e):
        return pl.BlockSpec(
            (None, pl.Element(COMBINE_WIN), pl.Element(d)),
            lambda t, w0: (e, pl.multiple_of(w0[t * N_EXPERTS + e], BF16_ROWS), 0))

    grid_spec = pltpu.PrefetchScalarGridSpec(
        num_scalar_prefetch=1,
        grid=(n // tt,),
        in_specs=[pl.BlockSpec((tt, d), lambda t, w0: (t, 0)),
                  pl.BlockSpec((N_EXPERTS, tt), lambda t, w0: (0, t)),
                  pl.BlockSpec((N_EXPERTS, tt), lambda t, w0: (0, t)),
                  pl.BlockSpec((4, d), lambda t, w0: (0, 0)),
                  pl.BlockSpec((None, 1, 6 * d), lambda t, w0: ((t * tt) // rows_per_mod, 0, 0))]
                 + [win_spec(e) for e in range(N_EXPERTS)],
        out_specs=pl.BlockSpec((tt, d), lambda t, w0: (t, 0)))
    return pl.pallas_call(
        _combine_kernel,
        grid_spec=grid_spec,
        out_shape=jax.ShapeDtypeStruct((n, d), F32),
        compiler_params=_params("arbitrary"),
        name="combine",
    )(w0_flat, x1, posm, aff_t, gains, mod_rows, *([ys] * N_EXPERTS))


def _slot_indices(idx_parts, n_seq, n):
    tok = (idx_parts[..., 0] * 64.0 + idx_parts[..., 1]).astype(I32)
    tok = tok + (jnp.arange(n_seq, dtype=I32) * n)[:, None, None]
    return jnp.transpose(tok, (1, 0, 2)).reshape(-1)


def _window_starts(posm, n_seq, n):
    cap = EC_CAPACITY_FACTOR * n // N_EXPERTS
    tt = COMBINE_TILE
    sel = (posm >= 0).reshape(N_EXPERTS, n_seq, n // tt, tt)
    cnt = jnp.sum(sel.astype(I32), axis=-1)
    before = jnp.cumsum(cnt, axis=-1) - cnt
    start = before + (jnp.arange(n_seq, dtype=I32) * cap)[None, :, None]
    start = (start // BF16_ROWS) * BF16_ROWS
    return jnp.transpose(start.reshape(N_EXPERTS, -1), (1, 0)).reshape(-1)


def _global_positions(posm, n_seq, n):
    cap = EC_CAPACITY_FACTOR * n // N_EXPERTS
    base = jnp.repeat(jnp.arange(n_seq, dtype=I32) * cap, n)[None, :]
    return jnp.where(posm >= 0, posm + base, -1)


def _channel_sublayer(x1, h_slabs, aff_t, mod_rows, gains, w_gate, w_up, w_down, layer, n_seq, n,
                      tf, tn=256):
    cap = EC_CAPACITY_FACTOR * n // N_EXPERTS
    posm, idx_parts = _route(aff_t, n_seq, n)
    idx_flat = _slot_indices(idx_parts, n_seq, n)
    ys = _expert_ffn(idx_flat, h_slabs, w_gate, w_up, w_down, layer, n_seq * cap, tf, tn)
    w0 = _window_starts(posm, n_seq, n)
    return _combine(w0, x1, _global_positions(posm, n_seq, n), aff_t, gains, mod_rows, ys)


def kernel(x_prompt, x_sample, c, cache_na_k, cache_na_v, cache_win_k, cache_win_v, c_ctx, w_ada, b_ada, norm_gain, w_in, na_rpb, win_sink, pool_w, pool_scale, w_branch, w_out, w_router, w_e_gate, w_e_up, w_e_down):
    batch, seq, d = x_prompt.shape
    dec_batch, dec_seq, _ = x_sample.shape
    depth = w_in.shape[0]
    past = cache_na_k.shape[2]
    assert d == D_MODEL and dec_seq % GRID_W == 0 and dec_batch <= SUBLANES - 1

    cvecs = jnp.zeros((SUBLANES, d), F32).at[0].set(c_ctx).at[1:1 + dec_batch].set(c)
    mods = _modulation(cvecs, w_ada, b_ada)

    cos_r, sin_r = _rope_tables(dec_seq)
    cos_ctx, sin_ctx = _dft_tables(seq)
    cos_lat, sin_lat = _dft_tables(dec_seq)
    cos_c, sin_c = _dft_tables(FNET_GROUP_W)

    xp = x_prompt.reshape(batch * seq, d)
    xs = x_sample.reshape(dec_batch * dec_seq, d)
    na_k, na_v, win_k, win_v = [], [], [], []
    no_sink = jnp.full((NA_HEADS,), NEG_INF, F32)
    for l in range(depth):
        gains = norm_gain[l]
        wb = w_branch[l].astype(BF16)
        wo = w_out[l].astype(BF16)
        wr_t = w_router[l].T
        mod_ctx = mods[l, 0:1].reshape(1, 1, 6 * d)
        mod_lat = mods[l, 1:1 + dec_batch].reshape(dec_batch, 1, 6 * d)

        ps, gl = _in_projection(xp, gains[0], mod_ctx, w_in, l)
        y_attn = _ctx_attention(ps, jnp.concatenate([no_sink, win_sink[l]]), batch, seq)
        y_pool = _pool_mix(ps, pool_w[l], pool_scale[l], batch, seq)
        y_fnet = _fourier_mix(ps, cos_ctx, sin_ctx, cos_c, sin_c, batch, seq)
        x1, h_slabs, aff_t = _merge(xp, [(y_attn, 0), (y_pool, 0), (y_attn, 1), (y_fnet, 0)],
                                    gl, wb, wo, gains, mod_ctx, wr_t)
        xp = _channel_sublayer(x1, h_slabs, aff_t, mod_ctx, gains, w_e_gate, w_e_up, w_e_down,
                               l, batch, seq, 512)
        hd = HEAD_DIM
        na_k.append(ps[:, COL_NA_K * hd:(COL_NA_K + NA_HEADS) * hd])
        na_v.append(ps[:, COL_NA_V * hd:(COL_NA_V + NA_HEADS) * hd])
        win_k.append(ps[:, COL_WIN_K * hd:(COL_WIN_K + WIN_KV_HEADS) * hd])
        win_v.append(ps[:, COL_WIN_V * hd:(COL_WIN_V + WIN_KV_HEADS) * hd])

        ps, gl = _in_projection(xs, gains[0], mod_lat, w_in, l)
        table = _na_bias_table(na_rpb[l], dec_seq // GRID_W)
        y_na = _latent_na(ps, cache_na_k[:, l].reshape(dec_batch, past, NA_HEADS * hd),
                          cache_na_v[:, l].reshape(dec_batch, past, NA_HEADS * hd),
                          table, dec_batch, dec_seq)
        y_win = _latent_win(ps, cache_win_k[:, l].reshape(dec_batch, past, WIN_KV_HEADS * hd),
                            cache_win_v[:, l].reshape(dec_batch, past, WIN_KV_HEADS * hd),
                            win_sink[l], cos_r, sin_r, dec_batch, dec_seq)
        y_pool = _pool_mix(ps, pool_w[l], pool_scale[l], dec_batch, dec_seq)
        y_fnet = _fourier_mix(ps, cos_lat, sin_lat, cos_c, sin_c, dec_batch, dec_seq)
        x1, h_slabs, aff_t = _merge(xs, [(y_na, 0), (y_pool, 0), (y_win, 0), (y_fnet, 0)],
                                    gl, wb, wo, gains, mod_lat, wr_t)
        xs = _channel_sublayer(x1, h_slabs, aff_t, mod_lat, gains, w_e_gate, w_e_up, w_e_down,
                               l, dec_batch, dec_seq, 256)

    def stack(parts, heads):
        return jnp.stack([p.reshape(batch, seq, heads, HEAD_DIM) for p in parts], axis=1)

    return (xp.reshape(batch, seq, d), xs.reshape(dec_batch, dec_seq, d),
            stack(na_k, NA_HEADS), stack(na_v, NA_HEADS),
            stack(win_k, WIN_KV_HEADS), stack(win_v, WIN_KV_HEADS))
```

```python
import functools

import numpy as np
import jax
import jax.numpy as jnp
from jax import lax
from jax.experimental import pallas as pl
from jax.experimental.pallas import tpu as pltpu

F32 = jnp.float32
BF16 = jnp.bfloat16
I32 = jnp.int32

D_MODEL = 2048
GRID_W = 64
HEAD_DIM = 128
N_BRANCH = 4
BRANCH_W = 512
NA_HEADS = 4
NA_MAX_ROWS = 8
NA_COLS = 16
WIN_HEADS = 4
WIN_KV_HEADS = 2
WIN_RADIUS = 128
WIN_BLOCK = 128
POOL_GROUPS = 4
POOL_GROUP_W = 128
POOL_WINDOWS = (2, 4, 8, 16)
FNET_GROUPS = 4
FNET_GROUP_W = 128
N_EXPERTS = 16
EC_CAPACITY_FACTOR = 2
ROPE_THETA = 10000.0
NORM_EPS = 1e-6
NEG_INF = -1e30
ATTN_SCALE = HEAD_DIM ** -0.5

SMALL_COLS = 3584
COL_NA_Q, COL_NA_K, COL_NA_V = 0, 4, 8
COL_POOL = 12
COL_WIN_Q, COL_WIN_K, COL_WIN_V = 16, 20, 22
COL_FNET = 24
GATE_COLS = N_BRANCH * D_MODEL

VMEM_LIMIT_BYTES = 56 * 1024 * 1024
LANES = 128
SUBLANES = 8
BF16_ROWS = 16

H_SLAB_W = LANES
H_SLABS = D_MODEL // (2 * LANES)


def _params(*sem):
    return pltpu.CompilerParams(dimension_semantics=sem, vmem_limit_bytes=VMEM_LIMIT_BYTES)


def _dot(a, b):
    return jnp.dot(a.astype(BF16), b.astype(BF16), preferred_element_type=F32)


def _dot_nt(a, b):
    return lax.dot_general(a.astype(BF16), b.astype(BF16), (((1,), (1,)), ((), ())),
                           preferred_element_type=F32)


def _rms(x, gain):
    return x * lax.rsqrt(jnp.mean(x * x, axis=-1, keepdims=True) + NORM_EPS) * gain


def _mod_kernel(c_ref, w_ref, b_ref, o_ref):
    c = c_ref[...]
    s = c * jax.nn.sigmoid(c)
    o_ref[...] = jnp.dot(s, w_ref[...], preferred_element_type=F32,
                         precision=lax.Precision.HIGHEST) + b_ref[...]


def _modulation(cvecs, w_ada, b_ada):
    depth, d, n6 = w_ada.shape
    tn = 1024
    return pl.pallas_call(
        _mod_kernel,
        grid=(depth, n6 // tn),
        in_specs=[pl.BlockSpec((SUBLANES, d), lambda l, j: (0, 0)),
                  pl.BlockSpec((None, d, tn), lambda l, j: (l, 0, j)),
                  pl.BlockSpec((None, 1, tn), lambda l, j: (l, 0, j))],
        out_specs=pl.BlockSpec((None, SUBLANES, tn), lambda l, j: (l, 0, j)),
        out_shape=jax.ShapeDtypeStruct((depth, SUBLANES, n6), F32),
        compiler_params=_params("arbitrary", "arbitrary"),
        name="modulation",
    )(cvecs, w_ada, b_ada.reshape(depth, 1, n6))


def _inproj_kernel(x_ref, g_ref, mod_ref, w_ref, ps_ref, gl_ref, h_scr, *, n_small):
    j = pl.program_id(1)

    @pl.when(j == 0)
    def _():
        d = D_MODEL
        h = _rms(x_ref[...], g_ref[...]) * (1.0 + mod_ref[:, d:2 * d]) + mod_ref[:, 0:d]
        h_scr[...] = h.astype(BF16)

    r = jnp.dot(h_scr[...], w_ref[...].astype(BF16), preferred_element_type=F32)

    @pl.when(j < n_small)
    def _():
        ps_ref[...] = r

    @pl.when(j >= n_small)
    def _():
        gl_ref[...] = r.astype(BF16)


def _in_projection(x, gain, mod_rows, w_in, layer):
    n, d = x.shape
    tm, tn = 1024, 512
    n_small = SMALL_COLS // tn
    n_tiles = w_in.shape[2] // tn
    rows_per_mod = n // mod_rows.shape[0]
    return pl.pallas_call(
        functools.partial(_inproj_kernel, n_small=n_small),
        grid=(n // tm, n_tiles),
        in_specs=[pl.BlockSpec((tm, d), lambda i, j: (i, 0)),
                  pl.BlockSpec((1, d), lambda i, j: (0, 0)),
                  pl.BlockSpec((None, 1, 6 * d), lambda i, j: ((i * tm) // rows_per_mod, 0, 0)),
                  pl.BlockSpec((None, d, tn), lambda i, j: (layer, 0, j))],
        out_specs=[pl.BlockSpec((tm, tn), lambda i, j: (i, jnp.minimum(j, n_small - 1))),
                   pl.BlockSpec((tm, tn), lambda i, j: (i, jnp.maximum(j - n_small, 0)))],
        out_shape=[jax.ShapeDtypeStruct((n, SMALL_COLS), F32),
                   jax.ShapeDtypeStruct((n, GATE_COLS), BF16)],
        scratch_shapes=[pltpu.VMEM((tm, d), BF16)],
        compiler_params=_params("arbitrary", "arbitrary"),
        name="in_projection",
    )(x, gain.reshape(1, d), mod_rows, w_in)


def _ctx_attn_kernel(sink_ref, q_ref, k_ref, v_ref, o_ref):
    sk = sink_ref[pl.program_id(1)]
    s = _dot_nt(q_ref[...], k_ref[...]) * ATTN_SCALE
    m = jnp.maximum(jnp.max(s, axis=-1, keepdims=True), sk)
    p = jnp.exp(s - m)
    den = jnp.sum(p, axis=-1, keepdims=True) + jnp.exp(sk - m)
    o_ref[...] = (_dot(p, v_ref[...]) / den).astype(o_ref.dtype)


def _ctx_attention(ps, sinks8, batch, seq):
    n_heads = NA_HEADS + WIN_HEADS
    group = WIN_HEADS // WIN_KV_HEADS

    def qcol(h):
        return jnp.where(h < NA_HEADS, COL_NA_Q + h, COL_WIN_Q + h - NA_HEADS)

    def kcol(h):
        return jnp.where(h < NA_HEADS, COL_NA_K + h, COL_WIN_K + (h - NA_HEADS) // group)

    def vcol(h):
        return jnp.where(h < NA_HEADS, COL_NA_V + h, COL_WIN_V + (h - NA_HEADS) // group)

    blk = (seq, HEAD_DIM)
    return pl.pallas_call(
        _ctx_attn_kernel,
        grid=(batch, n_heads),
        in_specs=[pl.BlockSpec(memory_space=pltpu.SMEM),
                  pl.BlockSpec(blk, lambda b, h: (b, qcol(h))),
                  pl.BlockSpec(blk, lambda b, h: (b, kcol(h))),
                  pl.BlockSpec(blk, lambda b, h: (b, vcol(h)))],
        out_specs=pl.BlockSpec(blk, lambda b, h: (b, h)),
        out_shape=jax.ShapeDtypeStruct((batch * seq, n_heads * HEAD_DIM), BF16),
        compiler_params=_params("arbitrary", "arbitrary"),
        name="ctx_attention",
    )(sinks8, ps, ps, ps)


NA_QROWS = 4
NA_QBLK = NA_QROWS * GRID_W
NA_KBLK = 3 * NA_QBLK


def _na_bias_table(rpb, rows):
    n_blocks = rows // NA_QROWS
    jrep = np.array([0, 1, n_blocks - 1])
    rr = np.arange(NA_QROWS)
    qc = np.arange(GRID_W)
    kk = np.arange(3 * NA_QROWS)
    kc = np.arange(GRID_W)
    r = NA_QROWS * jrep[:, None] + rr[None, :]
    krow = NA_QROWS * (jrep[:, None] - 1) + kk[None, :]
    row_start = np.clip(r - NA_MAX_ROWS // 2, 0, rows - NA_MAX_ROWS)
    kr3 = krow[:, None, :]
    vrow = ((kr3 >= row_start[:, :, None]) & (kr3 < row_start[:, :, None] + NA_MAX_ROWS)
            & (kr3 >= 0) & (kr3 < rows))
    dr = np.clip(kr3 - r[:, :, None] + NA_MAX_ROWS - 1, 0, 2 * NA_MAX_ROWS - 2)
    ws = np.clip(qc - NA_COLS // 2, 0, GRID_W - NA_COLS)
    vcol = (kc[None, :] >= ws[:, None]) & (kc[None, :] < ws[:, None] + NA_COLS)
    dc = np.clip(kc[None, :] - qc[:, None] + NA_COLS - 1, 0, 2 * NA_COLS - 2)
    n_dr, n_dc = 2 * NA_MAX_ROWS - 1, 2 * NA_COLS - 1
    sel_r = (dr.reshape(-1)[:, None] == np.arange(n_dr)[None, :]).astype(np.float32)
    sel_c = (np.arange(n_dc)[:, None] == dc.reshape(-1)[None, :]).astype(np.float32)
    rows_sel = jnp.einsum("rd,hdc->hrc", sel_r, rpb.astype(F32), precision=lax.Precision.HIGHEST)
    bias = jnp.einsum("hrc,cq->hrq", rows_sel, sel_c, precision=lax.Precision.HIGHEST)
    bias = bias.reshape(rpb.shape[0], 3, NA_QROWS, 3 * NA_QROWS, GRID_W, GRID_W)
    bias = jnp.transpose(bias, (0, 1, 2, 4, 3, 5))
    valid = vrow[:, :, None, :, None] & vcol[None, None, :, None, :]
    tbl = jnp.where(valid[None], bias, NEG_INF)
    return tbl.reshape(rpb.shape[0], 3, NA_QBLK, NA_KBLK)


def _na_kernel(q_ref, kp_ref, kc_ref, kn_ref, vp_ref, vc_ref, vn_ref, kx_ref, vx_ref, t_ref, o_ref):
    q = q_ref[...].astype(BF16)
    k = jnp.concatenate([kp_ref[...], kc_ref[...], kn_ref[...]], axis=0)
    v = jnp.concatenate([vp_ref[...], vc_ref[...], vn_ref[...]], axis=0)
    s_loc = _dot_nt(q, k) * ATTN_SCALE + t_ref[...]
    s_ctx = _dot_nt(q, kx_ref[...]) * ATTN_SCALE
    m = jnp.maximum(jnp.max(s_loc, axis=-1, keepdims=True), jnp.max(s_ctx, axis=-1, keepdims=True))
    p_loc = jnp.exp(s_loc - m)
    p_ctx = jnp.exp(s_ctx - m)
    den = jnp.sum(p_loc, axis=-1, keepdims=True) + jnp.sum(p_ctx, axis=-1, keepdims=True)
    o = _dot(p_loc, v) + _dot(p_ctx, vx_ref[...])
    o_ref[...] = (o / den).astype(o_ref.dtype)


def _latent_na(ps, cache_k, cache_v, table, batch, n):
    nb = n // NA_QBLK
    past = cache_k.shape[1]
    blk = (NA_QBLK, HEAD_DIM)

    def cur(col):
        return pl.BlockSpec(blk, lambda b, h, j: (b * nb + j, col + h))

    def prev(col):
        return pl.BlockSpec(blk, lambda b, h, j: (b * nb + jnp.maximum(j - 1, 0), col + h))

    def nxt(col):
        return pl.BlockSpec(blk, lambda b, h, j: (b * nb + jnp.minimum(j + 1, nb - 1), col + h))

    cache_spec = pl.BlockSpec((None, past, HEAD_DIM), lambda b, h, j: (b, 0, h))
    table_spec = pl.BlockSpec(
        (None, None, NA_QBLK, NA_KBLK),
        lambda b, h, j: (h, jnp.where(j == 0, 0, jnp.where(j == nb - 1, 2, 1)), 0, 0))
    return pl.pallas_call(
        _na_kernel,
        grid=(batch, NA_HEADS, nb),
        in_specs=[cur(COL_NA_Q), prev(COL_NA_K), cur(COL_NA_K), nxt(COL_NA_K),
                  prev(COL_NA_V), cur(COL_NA_V), nxt(COL_NA_V),
                  cache_spec, cache_spec, table_spec],
        out_specs=pl.BlockSpec(blk, lambda b, h, j: (b * nb + j, h)),
        out_shape=jax.ShapeDtypeStruct((batch * n, NA_HEADS * HEAD_DIM), BF16),
        compiler_params=_params("arbitrary", "arbitrary", "arbitrary"),
        name="latent_na",
    )(ps, ps, ps, ps, ps, ps, ps, cache_k, cache_v, table)


def _rope_tables(n):
    t = jnp.arange(n)
    row = (t // GRID_W).astype(F32)
    col = (t % GRID_W).astype(F32)
    n_freq = HEAD_DIM // 4
    inv = ROPE_THETA ** (-jnp.arange(n_freq, dtype=F32) / n_freq)
    ang = jnp.concatenate([row[:, None] * inv, col[:, None] * inv], axis=-1)
    cos = jnp.repeat(jnp.cos(ang), 2, axis=-1)
    sin = jnp.repeat(jnp.sin(ang), 2, axis=-1)
    sign = jnp.tile(jnp.array([-1.0, 1.0], F32), HEAD_DIM // 2)
    return cos, sin * sign


def _rope(x, cos, sin_signed):
    lane = lax.broadcasted_iota(I32, x.shape, 1)
    partner = jnp.where(lane % 2 == 0, pltpu.roll(x, HEAD_DIM - 1, 1), pltpu.roll(x, 1, 1))
    return x * cos + partner * sin_signed


def _win_kernel(sink_ref, q_ref, kp_ref, kc_ref, kn_ref, vp_ref, vc_ref, vn_ref, kx_ref, vx_ref,
                cq_ref, sq_ref, cp_ref, sp_ref, cc_ref, sc_ref, cn_ref, sn_ref, o_ref, *, nb):
    kv = pl.program_id(1)
    j = pl.program_id(2)
    bs = WIN_BLOCK
    group = WIN_HEADS // WIN_KV_HEADS
    qb = q_ref[...]
    q = jnp.concatenate([qb[:, g * HEAD_DIM:(g + 1) * HEAD_DIM] for g in range(group)], axis=0)
    cq = jnp.concatenate([cq_ref[...]] * group, axis=0)
    sq = jnp.concatenate([sq_ref[...]] * group, axis=0)
    q_rot = _rope(q, cq, sq)
    k = jnp.concatenate([_rope(kp_ref[...], cp_ref[...], sp_ref[...]),
                         _rope(kc_ref[...], cc_ref[...], sc_ref[...]),
                         _rope(kn_ref[...], cn_ref[...], sn_ref[...])], axis=0)
    v = jnp.concatenate([vp_ref[...], vc_ref[...], vn_ref[...]], axis=0)
    s_loc = _dot_nt(q_rot, k) * ATTN_SCALE
    qq = lax.broadcasted_iota(I32, s_loc.shape, 0) % bs
    kk = lax.broadcasted_iota(I32, s_loc.shape, 1)
    valid = jnp.abs(kk - bs - qq) <= WIN_RADIUS
    valid = valid & ((kk >= bs) | (j > 0)) & ((kk < 2 * bs) | (j < nb - 1))
    s_loc = jnp.where(valid, s_loc, NEG_INF)
    s_ctx = _dot_nt(q, kx_ref[...]) * ATTN_SCALE
    rows = lax.broadcasted_iota(I32, (group * bs, 1), 0)
    sk = jnp.zeros((group * bs, 1), F32)
    for g in range(group):
        sk = jnp.where(rows // bs == g, sink_ref[kv * group + g], sk)
    m = jnp.maximum(jnp.maximum(jnp.max(s_loc, axis=-1, keepdims=True),
                                jnp.max(s_ctx, axis=-1, keepdims=True)), sk)
    p_loc = jnp.exp(s_loc - m)
    p_ctx = jnp.exp(s_ctx - m)
    den = (jnp.sum(p_loc, axis=-1, keepdims=True) + jnp.sum(p_ctx, axis=-1, keepdims=True)
           + jnp.exp(sk - m))
    o = (_dot(p_loc, v) + _dot(p_ctx, vx_ref[...])) / den
    o_ref[...] = jnp.concatenate([o[g * bs:(g + 1) * bs] for g in range(group)],
                                 axis=1).astype(o_ref.dtype)


def _latent_win(ps, cache_k, cache_v, sinks, cos, sin, batch, n):
    nb = n // WIN_BLOCK
    past = cache_k.shape[1]
    group = WIN_HEADS // WIN_KV_HEADS
    blk = (WIN_BLOCK, HEAD_DIM)

    def cur_idx(j):
        return j

    def prev_idx(j):
        return jnp.maximum(j - 1, 0)

    def next_idx(j):
        return jnp.minimum(j + 1, nb - 1)

    def tok(col, fn):
        return pl.BlockSpec(blk, lambda b, kv, j: (b * nb + fn(j), col + kv))

    def rope_spec(fn):
        return pl.BlockSpec(blk, lambda b, kv, j: (fn(j), 0))

    q_spec = pl.BlockSpec((WIN_BLOCK, group * HEAD_DIM),
                          lambda b, kv, j: (b * nb + j, COL_WIN_Q // group + kv))
    cache_spec = pl.BlockSpec((None, past, HEAD_DIM), lambda b, kv, j: (b, 0, kv))
    return pl.pallas_call(
        functools.partial(_win_kernel, nb=nb),
        grid=(batch, WIN_KV_HEADS, nb),
        in_specs=[pl.BlockSpec(memory_space=pltpu.SMEM), q_spec,
                  tok(COL_WIN_K, prev_idx), tok(COL_WIN_K, cur_idx), tok(COL_WIN_K, next_idx),
                  tok(COL_WIN_V, prev_idx), tok(COL_WIN_V, cur_idx), tok(COL_WIN_V, next_idx),
                  cache_spec, cache_spec,
                  rope_spec(cur_idx), rope_spec(cur_idx), rope_spec(prev_idx), rope_spec(prev_idx),
                  rope_spec(cur_idx), rope_spec(cur_idx), rope_spec(next_idx), rope_spec(next_idx)],
        out_specs=pl.BlockSpec((WIN_BLOCK, group * HEAD_DIM), lambda b, kv, j: (b * nb + j, kv)),
        out_shape=jax.ShapeDtypeStruct((batch * n, WIN_HEADS * HEAD_DIM), BF16),
        compiler_params=_params("arbitrary", "arbitrary", "arbitrary"),
        name="latent_win",
    )(sinks, ps, ps, ps, ps, ps, ps, ps, cache_k, cache_v,
      cos, sin, cos, sin, cos, sin, cos, sin)


POOL_TILE = 256
POOL_HALO = SUBLANES


def _pool_kernel(up_ref, uc_ref, un_ref, w_ref, sc_ref, o_ref, ext, *, tiles_per_seq, n):
    lt = pl.program_id(0) % tiles_per_seq
    tm = POOL_TILE
    gw = POOL_GROUP_W
    ext[0:POOL_HALO, :] = jnp.where(lt > 0, up_ref[...], 0.0)
    ext[POOL_HALO:POOL_HALO + tm, :] = uc_ref[...]
    ext[POOL_HALO + tm:2 * POOL_HALO + tm, :] = jnp.where(lt < tiles_per_seq - 1, un_ref[...], 0.0)
    pos = lt * tm + lax.broadcasted_iota(I32, (tm, 1), 0)
    for g in range(POOL_GROUPS):
        half = POOL_WINDOWS[g] // 2
        lanes = slice(g * gw, (g + 1) * gw)
        acc = ext[pl.ds(POOL_HALO - half, tm), lanes]
        for dlt in range(-half + 1, half):
            acc = acc + ext[pl.ds(POOL_HALO + dlt, tm), lanes]
        cnt = (jnp.minimum(pos + half, n) - jnp.maximum(pos - half, 0)).astype(F32)
        pooled = acc / cnt - uc_ref[:, lanes]
        y = _dot(pooled, w_ref[g]) * sc_ref[:, lanes]
        o_ref[:, lanes] = y.astype(o_ref.dtype)


def _pool_mix(ps, pool_w, pool_scale, n_seq, n):
    tm = POOL_TILE
    tps = n // tm
    n_tiles = n_seq * tps
    halo_per_tile = tm // POOL_HALO
    width = POOL_GROUPS * POOL_GROUP_W
    colb = COL_POOL * LANES // width
    last_halo = n_tiles * halo_per_tile - 1
    return pl.pallas_call(
        functools.partial(_pool_kernel, tiles_per_seq=tps, n=n),
        grid=(n_tiles,),
        in_specs=[pl.BlockSpec((POOL_HALO, width),
                               lambda t: (jnp.maximum(t * halo_per_tile - 1, 0), colb)),
                  pl.BlockSpec((tm, width), lambda t: (t, colb)),
                  pl.BlockSpec((POOL_HALO, width),
                               lambda t: (jnp.minimum((t + 1) * halo_per_tile, last_halo), colb)),
                  pl.BlockSpec((POOL_GROUPS, POOL_GROUP_W, POOL_GROUP_W), lambda t: (0, 0, 0)),
                  pl.BlockSpec((1, width), lambda t: (0, 0))],
        out_specs=pl.BlockSpec((tm, width), lambda t: (t, 0)),
        out_shape=jax.ShapeDtypeStruct((n_seq * n, width), BF16),
        scratch_shapes=[pltpu.VMEM((tm + 2 * POOL_HALO, width), F32)],
        compiler_params=_params("arbitrary"),
        name="pool_mix",
    )(ps, ps, ps, pool_w, pool_scale.reshape(1, width))


def _dft_tables(n):
    k = jnp.arange(n, dtype=I32)
    if n <= 1024:
        ang = ((k[:, None] * k[None, :]) % n).astype(F32) * (2.0 * np.pi / n)
        return jnp.cos(ang).astype(BF16), jnp.sin(ang).astype(BF16)
    lo = 64
    hi = n // lo
    t1 = jnp.arange(hi, dtype=I32)
    t0 = jnp.arange(lo, dtype=I32)
    a = ((k[:, None] * t1[None, :] * lo) % n).astype(F32) * (2.0 * np.pi / n)
    b = ((k[:, None] * t0[None, :]) % n).astype(F32) * (2.0 * np.pi / n)
    ca, sa, cb, sb = jnp.cos(a), jnp.sin(a), jnp.cos(b), jnp.sin(b)
    cos = ca[:, :, None] * cb[:, None, :] - sa[:, :, None] * sb[:, None, :]
    sin = sa[:, :, None] * cb[:, None, :] + ca[:, :, None] * sb[:, None, :]
    return cos.reshape(n, n).astype(BF16), sin.reshape(n, n).astype(BF16)


def _fnet_kernel(u_ref, cn_ref, sn_ref, cc_ref, sc_ref, o_ref, v1, v2, *, scale):
    @pl.when(pl.program_id(1) == 0)
    def _():
        gw = FNET_GROUP_W
        for g in range(FNET_GROUPS):
            lanes = slice(g * gw, (g + 1) * gw)
            ug = u_ref[:, lanes].astype(BF16)
            v1[:, lanes] = jnp.dot(ug, cc_ref[...], preferred_element_type=F32).astype(BF16)
            v2[:, lanes] = jnp.dot(ug, sc_ref[...], preferred_element_type=F32).astype(BF16)

    y = (jnp.dot(cn_ref[...], v1[...], preferred_element_type=F32)
         - jnp.dot(sn_ref[...], v2[...], preferred_element_type=F32))
    o_ref[...] = (y * scale).astype(o_ref.dtype)


def _fourier_mix(ps, cos_n, sin_n, cos_c, sin_c, n_seq, n):
    width = FNET_GROUPS * FNET_GROUP_W
    colb = COL_FNET * LANES // width
    tm = min(n, 512)
    nt = n // tm
    scale = float(1.0 / np.sqrt(n * FNET_GROUP_W))
    return pl.pallas_call(
        functools.partial(_fnet_kernel, scale=scale),
        grid=(n_seq, nt),
        in_specs=[pl.BlockSpec((n, width), lambda b, i: (b, colb)),
                  pl.BlockSpec((tm, n), lambda b, i: (i, 0)),
                  pl.BlockSpec((tm, n), lambda b, i: (i, 0)),
                  pl.BlockSpec((FNET_GROUP_W, FNET_GROUP_W), lambda b, i: (0, 0)),
                  pl.BlockSpec((FNET_GROUP_W, FNET_GROUP_W), lambda b, i: (0, 0))],
        out_specs=pl.BlockSpec((tm, width), lambda b, i: (b * nt + i, 0)),
        out_shape=jax.ShapeDtypeStruct((n_seq * n, width), BF16),
        scratch_shapes=[pltpu.VMEM((n, width), BF16), pltpu.VMEM((n, width), BF16)],
        compiler_params=_params("arbitrary", "arbitrary"),
        name="fourier_mix",
    )(ps, cos_n, sin_n, cos_c, sin_c)


MERGE_TILE = 256
MERGE_SUBTILES = 2


def _merge_kernel(x_ref, y0_ref, y1_ref, y2_ref, y3_ref, gl_ref, wb_ref, wo_ref, gain_ref, mod_ref,
                  wr_ref, x1_ref, h_ref, aff_ref):
    d = D_MODEL
    ts = x_ref.shape[0] // MERGE_SUBTILES
    for sub in range(MERGE_SUBTILES):
        rows = slice(sub * ts, (sub + 1) * ts)
        acc = None
        for i, y_ref in enumerate((y0_ref, y1_ref, y2_ref, y3_ref)):
            proj = jnp.dot(y_ref[rows, :], wb_ref[i], preferred_element_type=F32)
            gate = jax.nn.sigmoid(gl_ref[rows, i * d:(i + 1) * d].astype(F32))
            acc = gate * proj if acc is None else acc + gate * proj
        mix = jnp.dot(acc.astype(BF16), wo_ref[...], preferred_element_type=F32)
        x1 = x_ref[rows, :] + mod_ref[:, 2 * d:3 * d] * _rms(mix, gain_ref[1:2, :])
        x1_ref[rows, :] = x1
        h = _rms(x1, gain_ref[2:3, :]) * (1.0 + mod_ref[:, 4 * d:5 * d]) + mod_ref[:, 3 * d:4 * d]
        hbits = lax.bitcast_convert_type(h.astype(BF16).astype(F32), jnp.uint32)
        for s in range(H_SLABS):
            lo = hbits[:, (2 * s) * LANES:(2 * s + 1) * LANES] >> 16
            hi = hbits[:, (2 * s + 1) * LANES:(2 * s + 2) * LANES]
            h_ref[pl.ds(sub * ts * H_SLABS + s, ts, stride=H_SLABS), :] = hi | lo
        logits = lax.dot_general(wr_ref[...], h, (((1,), (1,)), ((), ())),
                                 preferred_element_type=F32, precision=lax.Precision.HIGHEST)
        logits = logits - jnp.max(logits, axis=0, keepdims=True)
        e = jnp.exp(logits)
        aff_ref[:, rows] = e / jnp.sum(e, axis=0, keepdims=True)


def _merge(x, ys, gl, wb, wo, gains, mod_rows, wr_t):
    n, d = x.shape
    tm = MERGE_TILE
    rows_per_mod = n // mod_rows.shape[0]
    const = dict(pipeline_mode=pl.Buffered(1))
    y_specs = [pl.BlockSpec((tm, BRANCH_W), functools.partial(lambda i, c: (i, c), c=cb))
               for _, cb in ys]
    return pl.pallas_call(
        _merge_kernel,
        grid=(n // tm,),
        in_specs=[pl.BlockSpec((tm, d), lambda i: (i, 0))] + y_specs + [
            pl.BlockSpec((tm, GATE_COLS), lambda i: (i, 0)),
            pl.BlockSpec((N_BRANCH, BRANCH_W, d), lambda i: (0, 0, 0), **const),
            pl.BlockSpec((d, d), lambda i: (0, 0), **const),
            pl.BlockSpec((4, d), lambda i: (0, 0)),
            pl.BlockSpec((None, 1, 6 * d), lambda i: ((i * tm) // rows_per_mod, 0, 0)),
            pl.BlockSpec((N_EXPERTS, d), lambda i: (0, 0))],
        out_specs=[pl.BlockSpec((tm, d), lambda i: (i, 0)),
                   pl.BlockSpec((tm * H_SLABS, H_SLAB_W), lambda i: (i, 0)),
                   pl.BlockSpec((N_EXPERTS, tm), lambda i: (0, i))],
        out_shape=[jax.ShapeDtypeStruct((n, d), F32),
                   jax.ShapeDtypeStruct((n * H_SLABS, H_SLAB_W), jnp.uint32),
                   jax.ShapeDtypeStruct((N_EXPERTS, n), F32)],
        compiler_params=_params("arbitrary"),
        name="merge",
    )(x, *[a for a, _ in ys], gl, wb, wo, gains, mod_rows, wr_t)


CUMSUM_CHUNK = 256


def _lane_cumsum(x, upper):
    n = x.shape[1]
    carry = jnp.zeros((x.shape[0], 1), F32)
    out = []
    for c in range(n // CUMSUM_CHUNK):
        xc = x[:, c * CUMSUM_CHUNK:(c + 1) * CUMSUM_CHUNK]
        out.append(jnp.dot(xc.astype(BF16), upper, preferred_element_type=F32) + carry)
        carry = carry + jnp.sum(xc, axis=1, keepdims=True)
    return jnp.concatenate(out, axis=1) if len(out) > 1 else out[0]


def _route_kernel(a_ref, posm_ref, idx_ref, *, n, cap):
    bits = lax.bitcast_convert_type(a_ref[...], I32)
    thr = jnp.zeros((N_EXPERTS, 1), I32)
    for bit in range(30, -1, -1):
        cand = thr | (1 << bit)
        cnt = jnp.sum((bits >= cand).astype(F32), axis=1, keepdims=True)
        thr = jnp.where(cnt >= cap, cand, thr)
    gt = bits > thr
    eq = bits == thr
    need = cap - jnp.sum(gt.astype(F32), axis=1, keepdims=True)
    r = lax.broadcasted_iota(I32, (CUMSUM_CHUNK, CUMSUM_CHUNK), 0)
    c = lax.broadcasted_iota(I32, (CUMSUM_CHUNK, CUMSUM_CHUNK), 1)
    upper = (r <= c).astype(BF16)
    cum_eq = _lane_cumsum(eq.astype(F32), upper)
    sel = gt | (eq & (cum_eq <= need))
    pos = _lane_cumsum(sel.astype(F32), upper) - 1.0
    posm = jnp.where(sel, pos, -1.0).astype(I32)
    posm_ref[...] = posm
    tok = lax.broadcasted_iota(I32, (n, LANES), 0)
    lane = lax.broadcasted_iota(I32, (n, LANES), 1)
    tmat = jnp.where(lane == 0, tok // 64, jnp.where(lane == 1, tok % 64, 0)).astype(BF16)
    slot = lax.broadcasted_iota(I32, (cap, n), 0)
    for e in range(N_EXPERTS):
        onehot = (posm[e:e + 1, :] == slot).astype(BF16)
        idx_ref[e] = jnp.dot(onehot, tmat, preferred_element_type=F32)


def _route(aff_t, n_seq, n):
    cap = EC_CAPACITY_FACTOR * n // N_EXPERTS
    return pl.pallas_call(
        functools.partial(_route_kernel, n=n, cap=cap),
        grid=(n_seq,),
        in_specs=[pl.BlockSpec((N_EXPERTS, n), lambda b: (0, b))],
        out_specs=[pl.BlockSpec((N_EXPERTS, n), lambda b: (0, b)),
                   pl.BlockSpec((None, N_EXPERTS, cap, LANES), lambda b: (b, 0, 0, 0))],
        out_shape=[jax.ShapeDtypeStruct((N_EXPERTS, n_seq * n), I32),
                   jax.ShapeDtypeStruct((n_seq, N_EXPERTS, cap, LANES), F32)],
        compiler_params=_params("arbitrary"),
        name="route",
    )(aff_t)


COMBINE_TILE = 128
COMBINE_WIN = COMBINE_TILE + BF16_ROWS


def _expert_kernel(idx_ref, hc_ref, hl_ref, wg_ref, wu_ref, wd_ref, o_ref, xg, xb, hmid, sem,
                   *, mc, ml, nf):
    e = pl.program_id(0)
    f = pl.program_id(1)
    n_exp = pl.num_programs(0)
    m = mc + ml
    cc, cl = mc // nf, ml // nf

    def start_row(src_ref, expert, p):
        src = pl.multiple_of(idx_ref[expert * m + p] * H_SLABS, H_SLABS)
        dst = pl.multiple_of(p * H_SLABS, H_SLABS)
        pltpu.make_async_copy(src_ref.at[pl.ds(src, H_SLABS), :],
                              xg.at[pl.ds(dst, H_SLABS), :], sem).start()

    def wait_rows():
        pltpu.make_async_copy(hl_ref.at[pl.ds(0, m * H_SLABS), :], xg, sem).wait()

    @pl.when(f == 0)
    def _():
        @pl.when(e == 0)
        def _():
            def issue_ctx(p, carry):
                start_row(hc_ref, 0, p)
                return carry

            def issue_lat(p, carry):
                start_row(hl_ref, 0, mc + p)
                return carry

            lax.fori_loop(0, mc, issue_ctx, 0, unroll=8)
            lax.fori_loop(0, ml, issue_lat, 0, unroll=8)

        wait_rows()
        for s in range(H_SLABS):
            u = xg[pl.ds(s, m, stride=H_SLABS), :]
            even = lax.bitcast_convert_type(u << 16, F32)
            odd = lax.bitcast_convert_type(u & jnp.uint32(0xFFFF0000), F32)
            xb[:, (2 * s) * LANES:(2 * s + 1) * LANES] = even.astype(BF16)
            xb[:, (2 * s + 1) * LANES:(2 * s + 2) * LANES] = odd.astype(BF16)

    @pl.when(f < nf)
    def _():
        nxt = jnp.where(e + 1 < n_exp, e + 1, 0)
        for k in range(cc):
            start_row(hc_ref, nxt, f * cc + k)
        for k in range(cl):
            start_row(hl_ref, nxt, mc + f * cl + k)
        x = xb[...]
        a = jnp.dot(x, wg_ref[...].astype(BF16), preferred_element_type=F32)
        b = jnp.dot(x, wu_ref[...].astype(BF16), preferred_element_type=F32)
        hmid[f] = (a * jax.nn.sigmoid(a) * b).astype(BF16)

    @pl.when(f >= nf)
    def _():
        tf = hmid.shape[2]
        out = None
        for k in range(nf):
            part = jnp.dot(hmid[k], wd_ref[k * tf:(k + 1) * tf, :].astype(BF16),
                           preferred_element_type=F32)
            out = part if out is None else out + part
        o_ref[0:m, :] = out.astype(o_ref.dtype)
        o_ref[m:, :] = jnp.zeros((o_ref.shape[0] - m, o_ref.shape[1]), o_ref.dtype)

    @pl.when((e == n_exp - 1) & (f == pl.num_programs(1) - 1))
    def _():
        wait_rows()


def _expert_ffn(idx_flat, h_ctx, h_lat, w_gate, w_up, w_down, layer, mc, ml, tf, tn):
    _, n_exp, d, ff = w_gate.shape
    m = mc + ml
    m_pad = m + COMBINE_WIN
    nf = ff // tf
    assert mc % nf == 0 and ml % nf == 0 and h_lat.shape[0] >= m * H_SLABS
    grid_spec = pltpu.PrefetchScalarGridSpec(
        num_scalar_prefetch=1,
        grid=(n_exp, nf + d // tn),
        in_specs=[pl.BlockSpec(memory_space=pl.ANY),
                  pl.BlockSpec(memory_space=pl.ANY),
                  pl.BlockSpec((None, None, d, tf),
                               lambda e, f, idx: (layer, e, 0, jnp.minimum(f, nf - 1))),
                  pl.BlockSpec((None, None, d, tf),
                               lambda e, f, idx: (layer, e, 0, jnp.minimum(f, nf - 1))),
                  pl.BlockSpec((None, None, ff, tn),
                               lambda e, f, idx: (layer, e, 0, jnp.maximum(f - nf, 0)))],
        out_specs=pl.BlockSpec((None, m_pad, tn), lambda e, f, idx: (e, 0, jnp.maximum(f - nf, 0))),
        scratch_shapes=[pltpu.VMEM((m * H_SLABS, H_SLAB_W), jnp.uint32),
                        pltpu.VMEM((m, d), BF16),
                        pltpu.VMEM((nf, m, tf), BF16),
                        pltpu.SemaphoreType.DMA(())])
    return pl.pallas_call(
        functools.partial(_expert_kernel, mc=mc, ml=ml, nf=nf),
        grid_spec=grid_spec,
        out_shape=jax.ShapeDtypeStruct((n_exp, m_pad, d), BF16),
        compiler_params=_params("arbitrary", "arbitrary"),
        name="expert_ffn",
    )(idx_flat, h_ctx, h_lat, w_gate, w_up, w_down)


def _combine_kernel(w0_ref, x_ref, posm_ref, aff_ref, gain_ref, mod_ref, *rest):
    win_refs = rest[:N_EXPERTS]
    o_ref = rest[N_EXPERTS]
    t = pl.program_id(0)
    d = D_MODEL
    tt = COMBINE_TILE
    pad = jnp.zeros((LANES - 2 * N_EXPERTS, tt), F32)
    both = jnp.concatenate([posm_ref[...].astype(F32), aff_ref[...], pad], axis=0).T
    slot = lax.broadcasted_iota(I32, (tt, COMBINE_WIN), 1).astype(F32)
    ff = jnp.zeros((tt, d), F32)
    for e in range(N_EXPERTS):
        rel = both[:, e:e + 1] - w0_ref[t * N_EXPERTS + e].astype(F32)
        onehot = (rel == slot).astype(BF16)
        c = jnp.dot(onehot, win_refs[e][...], preferred_element_type=F32)
        ff = ff + both[:, N_EXPERTS + e:N_EXPERTS + e + 1] * c
    o_ref[...] = x_ref[...] + mod_ref[:, 5 * d:6 * d] * _rms(ff, gain_ref[3:4, :])


def _combine(w0_flat, x1, posm, aff_t, gains, mod_rows, ys):
    n, d = x1.shape
    tt = COMBINE_TILE
    rows_per_mod = n // mod_rows.shape[0]

    def win_spec(e):
        return pl.BlockSpec(
            (None, pl.Element(COMBINE_WIN), pl.Element(d)),
            lambda t, w0: (e, pl.multiple_of(w0[t * N_EXPERTS + e], BF16_ROWS), 0))

    grid_spec = pltpu.PrefetchScalarGridSpec(
        num_scalar_prefetch=1,
        grid=(n // tt,),
        in_specs=[pl.BlockSpec((tt, d), lambda t, w0: (t, 0)),
                  pl.BlockSpec((N_EXPERTS, tt), lambda t, w0: (0, t)),
                  pl.BlockSpec((N_EXPERTS, tt), lambda t, w0: (0, t)),
                  pl.BlockSpec((4, d), lambda t, w0: (0, 0)),
                  pl.BlockSpec((None, 1, 6 * d), lambda t, w0: ((t * tt) // rows_per_mod, 0, 0))]
                 + [win_spec(e) for e in range(N_EXPERTS)],
        out_specs=pl.BlockSpec((tt, d), lambda t, w0: (t, 0)))
    return pl.pallas_call(
        _combine_kernel,
        grid_spec=grid_spec,
        out_shape=jax.ShapeDtypeStruct((n, d), F32),
        compiler_params=_params("arbitrary"),
        name="combine",
    )(w0_flat, x1, posm, aff_t, gains, mod_rows, *([ys] * N_EXPERTS))


def _slot_indices(idx_parts, n_seq, n):
    tok = (idx_parts[..., 0] * 64.0 + idx_parts[..., 1]).astype(I32)
    tok = tok + (jnp.arange(n_seq, dtype=I32) * n)[:, None, None]
    return jnp.transpose(tok, (1, 0, 2)).reshape(N_EXPERTS, -1)


def _window_starts(posm, n_seq, n, base):
    cap = EC_CAPACITY_FACTOR * n // N_EXPERTS
    tt = COMBINE_TILE
    sel = (posm >= 0).reshape(N_EXPERTS, n_seq, n // tt, tt)
    cnt = jnp.sum(sel.astype(I32), axis=-1)
    before = jnp.cumsum(cnt, axis=-1) - cnt
    start = before + (jnp.arange(n_seq, dtype=I32) * cap)[None, :, None] + base
    start = (start // BF16_ROWS) * BF16_ROWS
    return jnp.transpose(start.reshape(N_EXPERTS, -1), (1, 0)).reshape(-1)


def _global_positions(posm, n_seq, n, base):
    cap = EC_CAPACITY_FACTOR * n // N_EXPERTS
    offs = jnp.repeat(jnp.arange(n_seq, dtype=I32) * cap, n)[None, :] + base
    return jnp.where(posm >= 0, posm + offs, -1)


FFN_HIDDEN_TILE = 256
FFN_OUT_TILE = 256


def _channel_sublayers(ctx, lat, gains, w_gate, w_up, w_down, layer):
    routed = []
    for (x1, h, aff_t, mod_rows, n_seq, n) in (ctx, lat):
        posm, idx_parts = _route(aff_t, n_seq, n)
        routed.append((posm, _slot_indices(idx_parts, n_seq, n)))
    mc, ml = routed[0][1].shape[1], routed[1][1].shape[1]
    idx_flat = jnp.concatenate([routed[0][1], routed[1][1]], axis=1).reshape(-1)
    ys = _expert_ffn(idx_flat, ctx[1], lat[1], w_gate, w_up, w_down, layer, mc, ml,
                     FFN_HIDDEN_TILE, FFN_OUT_TILE)
    outs = []
    for (x1, h, aff_t, mod_rows, n_seq, n), (posm, _), base in zip((ctx, lat), routed, (0, mc)):
        outs.append(_combine(_window_starts(posm, n_seq, n, base), x1,
                             _global_positions(posm, n_seq, n, base), aff_t, gains, mod_rows, ys))
    return outs


def kernel(x_prompt, x_sample, c, cache_na_k, cache_na_v, cache_win_k, cache_win_v, c_ctx, w_ada, b_ada, norm_gain, w_in, na_rpb, win_sink, pool_w, pool_scale, w_branch, w_out, w_router, w_e_gate, w_e_up, w_e_down):
    batch, seq, d = x_prompt.shape
    dec_batch, dec_seq, _ = x_sample.shape
    depth = w_in.shape[0]
    past = cache_na_k.shape[2]
    assert d == D_MODEL and dec_seq % GRID_W == 0 and dec_batch <= SUBLANES - 1

    cvecs = jnp.zeros((SUBLANES, d), F32).at[0].set(c_ctx).at[1:1 + dec_batch].set(c)
    mods = _modulation(cvecs, w_ada, b_ada)

    cos_r, sin_r = _rope_tables(dec_seq)
    cos_ctx, sin_ctx = _dft_tables(seq)
    cos_lat, sin_lat = _dft_tables(dec_seq)
    cos_c, sin_c = _dft_tables(FNET_GROUP_W)

    xp = x_prompt.reshape(batch * seq, d)
    xs = x_sample.reshape(dec_batch * dec_seq, d)
    na_k, na_v, win_k, win_v = [], [], [], []
    no_sink = jnp.full((NA_HEADS,), NEG_INF, F32)
    for l in range(depth):
        gains = norm_gain[l]
        wb = w_branch[l].astype(BF16)
        wo = w_out[l].astype(BF16)
        wr_t = w_router[l].T
        mod_ctx = mods[l, 0:1].reshape(1, 1, 6 * d)
        mod_lat = mods[l, 1:1 + dec_batch].reshape(dec_batch, 1, 6 * d)

        ps, gl = _in_projection(xp, gains[0], mod_ctx, w_in, l)
        y_attn = _ctx_attention(ps, jnp.concatenate([no_sink, win_sink[l]]), batch, seq)
        y_pool = _pool_mix(ps, pool_w[l], pool_scale[l], batch, seq)
        y_fnet = _fourier_mix(ps, cos_ctx, sin_ctx, cos_c, sin_c, batch, seq)
        merged_ctx = _merge(xp, [(y_attn, 0), (y_pool, 0), (y_attn, 1), (y_fnet, 0)],
                            gl, wb, wo, gains, mod_ctx, wr_t)
        hd = HEAD_DIM
        na_k.append(ps[:, COL_NA_K * hd:(COL_NA_K + NA_HEADS) * hd])
        na_v.append(ps[:, COL_NA_V * hd:(COL_NA_V + NA_HEADS) * hd])
        win_k.append(ps[:, COL_WIN_K * hd:(COL_WIN_K + WIN_KV_HEADS) * hd])
        win_v.append(ps[:, COL_WIN_V * hd:(COL_WIN_V + WIN_KV_HEADS) * hd])

        ps, gl = _in_projection(xs, gains[0], mod_lat, w_in, l)
        table = _na_bias_table(na_rpb[l], dec_seq // GRID_W)
        y_na = _latent_na(ps, cache_na_k[:, l].reshape(dec_batch, past, NA_HEADS * hd),
                          cache_na_v[:, l].reshape(dec_batch, past, NA_HEADS * hd),
                          table, dec_batch, dec_seq)
        y_win = _latent_win(ps, cache_win_k[:, l].reshape(dec_batch, past, WIN_KV_HEADS * hd),
                            cache_win_v[:, l].reshape(dec_batch, past, WIN_KV_HEADS * hd),
                            win_sink[l], cos_r, sin_r, dec_batch, dec_seq)
        y_pool = _pool_mix(ps, pool_w[l], pool_scale[l], dec_batch, dec_seq)
        y_fnet = _fourier_mix(ps, cos_lat, sin_lat, cos_c, sin_c, dec_batch, dec_seq)
        merged_lat = _merge(xs, [(y_na, 0), (y_pool, 0), (y_win, 0), (y_fnet, 0)],
                            gl, wb, wo, gains, mod_lat, wr_t)

        xp, xs = _channel_sublayers((*merged_ctx, mod_ctx, batch, seq),
                                    (*merged_lat, mod_lat, dec_batch, dec_seq),
                                    gains, w_e_gate, w_e_up, w_e_down, l)

    def stack(parts, heads):
        return jnp.stack([p.reshape(batch, seq, heads, HEAD_DIM) for p in parts], axis=1)

    return (xp.reshape(batch, seq, d), xs.reshape(dec_batch, dec_seq, d),
            stack(na_k, NA_HEADS), stack(na_v, NA_HEADS),
            stack(win_k, WIN_KV_HEADS), stack(win_v, WIN_KV_HEADS))
```

```python
import functools

import numpy as np
import jax
import jax.numpy as jnp
from jax import lax
from jax.experimental import pallas as pl
from jax.experimental.pallas import tpu as pltpu

F32 = jnp.float32
BF16 = jnp.bfloat16
I32 = jnp.int32

D_MODEL = 2048
GRID_W = 64
HEAD_DIM = 128
N_BRANCH = 4
BRANCH_W = 512
NA_HEADS = 4
NA_MAX_ROWS = 8
NA_COLS = 16
WIN_HEADS = 4
WIN_KV_HEADS = 2
WIN_RADIUS = 128
WIN_BLOCK = 128
POOL_GROUPS = 4
POOL_GROUP_W = 128
POOL_WINDOWS = (2, 4, 8, 16)
FNET_GROUPS = 4
FNET_GROUP_W = 128
N_EXPERTS = 16
EC_CAPACITY_FACTOR = 2
ROPE_THETA = 10000.0
NORM_EPS = 1e-6
NEG_INF = -1e30
ATTN_SCALE = HEAD_DIM ** -0.5

SMALL_COLS = 3584
COL_NA_Q, COL_NA_K, COL_NA_V = 0, 4, 8
COL_POOL = 12
COL_WIN_Q, COL_WIN_K, COL_WIN_V = 16, 20, 22
COL_FNET = 24
GATE_COLS = N_BRANCH * D_MODEL

VMEM_LIMIT_BYTES = 56 * 1024 * 1024
LANES = 128
SUBLANES = 8
BF16_ROWS = 16

H_SLAB_W = LANES
H_SLABS = D_MODEL // (2 * LANES)


def _params(*sem):
    return pltpu.CompilerParams(dimension_semantics=sem, vmem_limit_bytes=VMEM_LIMIT_BYTES)


def _dot(a, b):
    return jnp.dot(a.astype(BF16), b.astype(BF16), preferred_element_type=F32)


def _dot_nt(a, b):
    return lax.dot_general(a.astype(BF16), b.astype(BF16), (((1,), (1,)), ((), ())),
                           preferred_element_type=F32)


def _rms(x, gain):
    return x * lax.rsqrt(jnp.mean(x * x, axis=-1, keepdims=True) + NORM_EPS) * gain


def _mod_kernel(c_ref, w_ref, b_ref, o_ref):
    c = c_ref[...]
    s = c * jax.nn.sigmoid(c)
    o_ref[...] = jnp.dot(s, w_ref[...], preferred_element_type=F32,
                         precision=lax.Precision.HIGHEST) + b_ref[...]


def _modulation(cvecs, w_ada, b_ada):
    depth, d, n6 = w_ada.shape
    tn = 1024
    return pl.pallas_call(
        _mod_kernel,
        grid=(depth, n6 // tn),
        in_specs=[pl.BlockSpec((SUBLANES, d), lambda l, j: (0, 0)),
                  pl.BlockSpec((None, d, tn), lambda l, j: (l, 0, j)),
                  pl.BlockSpec((None, 1, tn), lambda l, j: (l, 0, j))],
        out_specs=pl.BlockSpec((None, SUBLANES, tn), lambda l, j: (l, 0, j)),
        out_shape=jax.ShapeDtypeStruct((depth, SUBLANES, n6), F32),
        compiler_params=_params("arbitrary", "arbitrary"),
        name="modulation",
    )(cvecs, w_ada, b_ada.reshape(depth, 1, n6))


def _inproj_kernel(x_ref, g_ref, mod_ref, w_ref, ps_ref, gl_ref, h_scr, *, n_small):
    j = pl.program_id(1)

    @pl.when(j == 0)
    def _():
        d = D_MODEL
        h = _rms(x_ref[...], g_ref[...]) * (1.0 + mod_ref[:, d:2 * d]) + mod_ref[:, 0:d]
        h_scr[...] = h.astype(BF16)

    r = jnp.dot(h_scr[...], w_ref[...].astype(BF16), preferred_element_type=F32)

    @pl.when(j < n_small)
    def _():
        ps_ref[...] = r

    @pl.when(j >= n_small)
    def _():
        gl_ref[...] = r.astype(BF16)


def _in_projection(x, gain, mod_rows, w_in, layer):
    n, d = x.shape
    tm, tn = 1024, 512
    n_small = SMALL_COLS // tn
    n_tiles = w_in.shape[2] // tn
    rows_per_mod = n // mod_rows.shape[0]
    return pl.pallas_call(
        functools.partial(_inproj_kernel, n_small=n_small),
        grid=(n // tm, n_tiles),
        in_specs=[pl.BlockSpec((tm, d), lambda i, j: (i, 0)),
                  pl.BlockSpec((1, d), lambda i, j: (0, 0)),
                  pl.BlockSpec((None, 1, 6 * d), lambda i, j: ((i * tm) // rows_per_mod, 0, 0)),
                  pl.BlockSpec((None, d, tn), lambda i, j: (layer, 0, j))],
        out_specs=[pl.BlockSpec((tm, tn), lambda i, j: (i, jnp.minimum(j, n_small - 1))),
                   pl.BlockSpec((tm, tn), lambda i, j: (i, jnp.maximum(j - n_small, 0)))],
        out_shape=[jax.ShapeDtypeStruct((n, SMALL_COLS), F32),
                   jax.ShapeDtypeStruct((n, GATE_COLS), BF16)],
        scratch_shapes=[pltpu.VMEM((tm, d), BF16)],
        compiler_params=_params("arbitrary", "arbitrary"),
        name="in_projection",
    )(x, gain.reshape(1, d), mod_rows, w_in)


def _full_attention(q, k, v, sink):
    s = _dot_nt(q, k) * ATTN_SCALE
    m = jnp.max(s, axis=-1, keepdims=True)
    if sink is not None:
        m = jnp.maximum(m, sink)
    p = jnp.exp(s - m)
    den = jnp.sum(p, axis=-1, keepdims=True)
    if sink is not None:
        den = den + jnp.exp(sink - m)
    return _dot(p, v) / den


def _ctx_attn_kernel(sink_ref, qa_ref, ka_ref, va_ref, qw_ref, kw_ref, vw_ref, o_ref):
    hd = HEAD_DIM
    group = WIN_HEADS // WIN_KV_HEADS
    for h in range(NA_HEADS):
        sl = slice(h * hd, (h + 1) * hd)
        o_ref[:, sl] = _full_attention(qa_ref[:, sl], ka_ref[:, sl], va_ref[:, sl],
                                       None).astype(o_ref.dtype)
    for h in range(WIN_HEADS):
        sl = slice(h * hd, (h + 1) * hd)
        kv = slice((h // group) * hd, (h // group + 1) * hd)
        o_ref[:, (NA_HEADS + h) * hd:(NA_HEADS + h + 1) * hd] = _full_attention(
            qw_ref[:, sl], kw_ref[:, kv], vw_ref[:, kv], sink_ref[h]).astype(o_ref.dtype)


def _ctx_attention(ps, sinks, batch, seq):
    n_heads = NA_HEADS + WIN_HEADS
    na_w = NA_HEADS * HEAD_DIM
    kv_w = WIN_KV_HEADS * HEAD_DIM

    def cols(width, first_lane_block):
        return pl.BlockSpec((seq, width), lambda b: (b, first_lane_block * LANES // width))

    return pl.pallas_call(
        _ctx_attn_kernel,
        grid=(batch,),
        in_specs=[pl.BlockSpec(memory_space=pltpu.SMEM),
                  cols(na_w, COL_NA_Q), cols(na_w, COL_NA_K), cols(na_w, COL_NA_V),
                  cols(na_w, COL_WIN_Q), cols(kv_w, COL_WIN_K), cols(kv_w, COL_WIN_V)],
        out_specs=pl.BlockSpec((seq, n_heads * HEAD_DIM), lambda b: (b, 0)),
        out_shape=jax.ShapeDtypeStruct((batch * seq, n_heads * HEAD_DIM), BF16),
        compiler_params=_params("arbitrary"),
        name="ctx_attention",
    )(sinks, ps, ps, ps, ps, ps, ps)


NA_QROWS = 4
NA_QBLK = NA_QROWS * GRID_W
NA_KBLK = 3 * NA_QBLK


def _na_bias_table(rpb, rows):
    n_blocks = rows // NA_QROWS
    jrep = np.array([0, 1, n_blocks - 1])
    rr = np.arange(NA_QROWS)
    qc = np.arange(GRID_W)
    kk = np.arange(3 * NA_QROWS)
    kc = np.arange(GRID_W)
    r = NA_QROWS * jrep[:, None] + rr[None, :]
    krow = NA_QROWS * (jrep[:, None] - 1) + kk[None, :]
    row_start = np.clip(r - NA_MAX_ROWS // 2, 0, rows - NA_MAX_ROWS)
    kr3 = krow[:, None, :]
    vrow = ((kr3 >= row_start[:, :, None]) & (kr3 < row_start[:, :, None] + NA_MAX_ROWS)
            & (kr3 >= 0) & (kr3 < rows))
    dr = np.clip(kr3 - r[:, :, None] + NA_MAX_ROWS - 1, 0, 2 * NA_MAX_ROWS - 2)
    ws = np.clip(qc - NA_COLS // 2, 0, GRID_W - NA_COLS)
    vcol = (kc[None, :] >= ws[:, None]) & (kc[None, :] < ws[:, None] + NA_COLS)
    dc = np.clip(kc[None, :] - qc[:, None] + NA_COLS - 1, 0, 2 * NA_COLS - 2)
    n_dr, n_dc = 2 * NA_MAX_ROWS - 1, 2 * NA_COLS - 1
    sel_r = (dr.reshape(-1)[:, None] == np.arange(n_dr)[None, :]).astype(np.float32)
    sel_c = (np.arange(n_dc)[:, None] == dc.reshape(-1)[None, :]).astype(np.float32)
    rows_sel = jnp.einsum("rd,hdc->hrc", sel_r, rpb.astype(F32), precision=lax.Precision.HIGHEST)
    bias = jnp.einsum("hrc,cq->hrq", rows_sel, sel_c, precision=lax.Precision.HIGHEST)
    bias = bias.reshape(rpb.shape[0], 3, NA_QROWS, 3 * NA_QROWS, GRID_W, GRID_W)
    bias = jnp.transpose(bias, (0, 1, 2, 4, 3, 5))
    valid = vrow[:, :, None, :, None] & vcol[None, None, :, None, :]
    tbl = jnp.where(valid[None], bias, NEG_INF)
    return tbl.reshape(rpb.shape[0], 3, NA_QBLK, NA_KBLK)


def _na_kernel(q_ref, kp_ref, kc_ref, kn_ref, vp_ref, vc_ref, vn_ref, kx_ref, vx_ref, t_ref, o_ref):
    for h in range(NA_HEADS):
        sl = slice(h * HEAD_DIM, (h + 1) * HEAD_DIM)
        q = q_ref[:, sl].astype(BF16)
        k = jnp.concatenate([kp_ref[:, sl], kc_ref[:, sl], kn_ref[:, sl]], axis=0)
        v = jnp.concatenate([vp_ref[:, sl], vc_ref[:, sl], vn_ref[:, sl]], axis=0)
        s_loc = _dot_nt(q, k) * ATTN_SCALE + t_ref[h]
        s_ctx = _dot_nt(q, kx_ref[:, sl]) * ATTN_SCALE
        m = jnp.maximum(jnp.max(s_loc, axis=-1, keepdims=True),
                        jnp.max(s_ctx, axis=-1, keepdims=True))
        p_loc = jnp.exp(s_loc - m)
        p_ctx = jnp.exp(s_ctx - m)
        den = jnp.sum(p_loc, axis=-1, keepdims=True) + jnp.sum(p_ctx, axis=-1, keepdims=True)
        o = _dot(p_loc, v) + _dot(p_ctx, vx_ref[:, sl])
        o_ref[:, sl] = (o / den).astype(o_ref.dtype)


def _latent_na(ps, cache_k, cache_v, table, batch, n):
    nb = n // NA_QBLK
    past = cache_k.shape[1]
    width = NA_HEADS * HEAD_DIM
    blk = (NA_QBLK, width)

    def cur(col):
        return pl.BlockSpec(blk, lambda b, j: (b * nb + j, col * LANES // width))

    def prev(col):
        return pl.BlockSpec(blk, lambda b, j: (b * nb + jnp.maximum(j - 1, 0), col * LANES // width))

    def nxt(col):
        return pl.BlockSpec(blk, lambda b, j: (b * nb + jnp.minimum(j + 1, nb - 1),
                                               col * LANES // width))

    cache_spec = pl.BlockSpec((None, past, width), lambda b, j: (b, 0, 0))
    table_spec = pl.BlockSpec(
        (NA_HEADS, None, NA_QBLK, NA_KBLK),
        lambda b, j: (0, jnp.where(j == 0, 0, jnp.where(j == nb - 1, 2, 1)), 0, 0))
    return pl.pallas_call(
        _na_kernel,
        grid=(batch, nb),
        in_specs=[cur(COL_NA_Q), prev(COL_NA_K), cur(COL_NA_K), nxt(COL_NA_K),
                  prev(COL_NA_V), cur(COL_NA_V), nxt(COL_NA_V),
                  cache_spec, cache_spec, table_spec],
        out_specs=pl.BlockSpec(blk, lambda b, j: (b * nb + j, 0)),
        out_shape=jax.ShapeDtypeStruct((batch * n, width), BF16),
        compiler_params=_params("arbitrary", "arbitrary"),
        name="latent_na",
    )(ps, ps, ps, ps, ps, ps, ps, cache_k, cache_v, table)


def _rope_tables(n):
    t = jnp.arange(n)
    row = (t // GRID_W).astype(F32)
    col = (t % GRID_W).astype(F32)
    n_freq = HEAD_DIM // 4
    inv = ROPE_THETA ** (-jnp.arange(n_freq, dtype=F32) / n_freq)
    ang = jnp.concatenate([row[:, None] * inv, col[:, None] * inv], axis=-1)
    cos = jnp.repeat(jnp.cos(ang), 2, axis=-1)
    sin = jnp.repeat(jnp.sin(ang), 2, axis=-1)
    sign = jnp.tile(jnp.array([-1.0, 1.0], F32), HEAD_DIM // 2)
    return cos, sin * sign


def _rope(x, cos, sin_signed):
    lane = lax.broadcasted_iota(I32, x.shape, 1)
    partner = jnp.where(lane % 2 == 0, pltpu.roll(x, HEAD_DIM - 1, 1), pltpu.roll(x, 1, 1))
    return x * cos + partner * sin_signed


def _win_kernel(sink_ref, q_ref, kp_ref, kc_ref, kn_ref, vp_ref, vc_ref, vn_ref, kx_ref, vx_ref,
                cq_ref, sq_ref, cp_ref, sp_ref, cc_ref, sc_ref, cn_ref, sn_ref, o_ref, *, nb):
    j = pl.program_id(1)
    bs = WIN_BLOCK
    hd = HEAD_DIM
    group = WIN_HEADS // WIN_KV_HEADS
    cq = jnp.concatenate([cq_ref[...]] * group, axis=0)
    sq = jnp.concatenate([sq_ref[...]] * group, axis=0)
    qq = lax.broadcasted_iota(I32, (group * bs, 3 * bs), 0) % bs
    kk = lax.broadcasted_iota(I32, (group * bs, 3 * bs), 1)
    valid = jnp.abs(kk - bs - qq) <= WIN_RADIUS
    valid = valid & ((kk >= bs) | (j > 0)) & ((kk < 2 * bs) | (j < nb - 1))
    rows = lax.broadcasted_iota(I32, (group * bs, 1), 0)
    for kv in range(WIN_KV_HEADS):
        sl = slice(kv * hd, (kv + 1) * hd)
        q = jnp.concatenate([q_ref[:, (kv * group + g) * hd:(kv * group + g + 1) * hd]
                             for g in range(group)], axis=0)
        q_rot = _rope(q, cq, sq)
        k = jnp.concatenate([_rope(kp_ref[:, sl], cp_ref[...], sp_ref[...]),
                             _rope(kc_ref[:, sl], cc_ref[...], sc_ref[...]),
                             _rope(kn_ref[:, sl], cn_ref[...], sn_ref[...])], axis=0)
        v = jnp.concatenate([vp_ref[:, sl], vc_ref[:, sl], vn_ref[:, sl]], axis=0)
        s_loc = jnp.where(valid, _dot_nt(q_rot, k) * ATTN_SCALE, NEG_INF)
        s_ctx = _dot_nt(q, kx_ref[:, sl]) * ATTN_SCALE
        sk = jnp.zeros((group * bs, 1), F32)
        for g in range(group):
            sk = jnp.where(rows // bs == g, sink_ref[kv * group + g], sk)
        m = jnp.maximum(jnp.maximum(jnp.max(s_loc, axis=-1, keepdims=True),
                                    jnp.max(s_ctx, axis=-1, keepdims=True)), sk)
        p_loc = jnp.exp(s_loc - m)
        p_ctx = jnp.exp(s_ctx - m)
        den = (jnp.sum(p_loc, axis=-1, keepdims=True) + jnp.sum(p_ctx, axis=-1, keepdims=True)
               + jnp.exp(sk - m))
        o = (_dot(p_loc, v) + _dot(p_ctx, vx_ref[:, sl])) / den
        for g in range(group):
            o_ref[:, (kv * group + g) * hd:(kv * group + g + 1) * hd] = (
                o[g * bs:(g + 1) * bs].astype(o_ref.dtype))


def _latent_win(ps, cache_k, cache_v, sinks, cos, sin, batch, n):
    nb = n // WIN_BLOCK
    past = cache_k.shape[1]
    q_w = WIN_HEADS * HEAD_DIM
    kv_w = WIN_KV_HEADS * HEAD_DIM

    def cur_idx(j):
        return j

    def prev_idx(j):
        return jnp.maximum(j - 1, 0)

    def next_idx(j):
        return jnp.minimum(j + 1, nb - 1)

    def tok(col, fn):
        return pl.BlockSpec((WIN_BLOCK, kv_w), lambda b, j: (b * nb + fn(j), col * LANES // kv_w))

    def rope_spec(fn):
        return pl.BlockSpec((WIN_BLOCK, HEAD_DIM), lambda b, j: (fn(j), 0))

    q_spec = pl.BlockSpec((WIN_BLOCK, q_w), lambda b, j: (b * nb + j, COL_WIN_Q * LANES // q_w))
    cache_spec = pl.BlockSpec((None, past, kv_w), lambda b, j: (b, 0, 0))
    return pl.pallas_call(
        functools.partial(_win_kernel, nb=nb),
        grid=(batch, nb),
        in_specs=[pl.BlockSpec(memory_space=pltpu.SMEM), q_spec,
                  tok(COL_WIN_K, prev_idx), tok(COL_WIN_K, cur_idx), tok(COL_WIN_K, next_idx),
                  tok(COL_WIN_V, prev_idx), tok(COL_WIN_V, cur_idx), tok(COL_WIN_V, next_idx),
                  cache_spec, cache_spec,
                  rope_spec(cur_idx), rope_spec(cur_idx), rope_spec(prev_idx), rope_spec(prev_idx),
                  rope_spec(cur_idx), rope_spec(cur_idx), rope_spec(next_idx), rope_spec(next_idx)],
        out_specs=pl.BlockSpec((WIN_BLOCK, q_w), lambda b, j: (b * nb + j, 0)),
        out_shape=jax.ShapeDtypeStruct((batch * n, q_w), BF16),
        compiler_params=_params("arbitrary", "arbitrary"),
        name="latent_win",
    )(sinks, ps, ps, ps, ps, ps, ps, ps, cache_k, cache_v,
      cos, sin, cos, sin, cos, sin, cos, sin)


POOL_TILE = 256
POOL_HALO = SUBLANES


def _pool_kernel(up_ref, uc_ref, un_ref, w_ref, sc_ref, o_ref, ext, *, tiles_per_seq, n):
    lt = pl.program_id(0) % tiles_per_seq
    tm = POOL_TILE
    gw = POOL_GROUP_W
    ext[0:POOL_HALO, :] = jnp.where(lt > 0, up_ref[...], 0.0)
    ext[POOL_HALO:POOL_HALO + tm, :] = uc_ref[...]
    ext[POOL_HALO + tm:2 * POOL_HALO + tm, :] = jnp.where(lt < tiles_per_seq - 1, un_ref[...], 0.0)
    pos = lt * tm + lax.broadcasted_iota(I32, (tm, 1), 0)
    for g in range(POOL_GROUPS):
        half = POOL_WINDOWS[g] // 2
        lanes = slice(g * gw, (g + 1) * gw)
        acc = ext[pl.ds(POOL_HALO - half, tm), lanes]
        for dlt in range(-half + 1, half):
            acc = acc + ext[pl.ds(POOL_HALO + dlt, tm), lanes]
        cnt = (jnp.minimum(pos + half, n) - jnp.maximum(pos - half, 0)).astype(F32)
        pooled = acc / cnt - uc_ref[:, lanes]
        y = _dot(pooled, w_ref[g]) * sc_ref[:, lanes]
        o_ref[:, lanes] = y.astype(o_ref.dtype)


def _pool_mix(ps, pool_w, pool_scale, n_seq, n):
    tm = POOL_TILE
    tps = n // tm
    n_tiles = n_seq * tps
    halo_per_tile = tm // POOL_HALO
    width = POOL_GROUPS * POOL_GROUP_W
    colb = COL_POOL * LANES // width
    last_halo = n_tiles * halo_per_tile - 1
    return pl.pallas_call(
        functools.partial(_pool_kernel, tiles_per_seq=tps, n=n),
        grid=(n_tiles,),
        in_specs=[pl.BlockSpec((POOL_HALO, width),
                               lambda t: (jnp.maximum(t * halo_per_tile - 1, 0), colb)),
                  pl.BlockSpec((tm, width), lambda t: (t, colb)),
                  pl.BlockSpec((POOL_HALO, width),
                               lambda t: (jnp.minimum((t + 1) * halo_per_tile, last_halo), colb)),
                  pl.BlockSpec((POOL_GROUPS, POOL_GROUP_W, POOL_GROUP_W), lambda t: (0, 0, 0)),
                  pl.BlockSpec((1, width), lambda t: (0, 0))],
        out_specs=pl.BlockSpec((tm, width), lambda t: (t, 0)),
        out_shape=jax.ShapeDtypeStruct((n_seq * n, width), BF16),
        scratch_shapes=[pltpu.VMEM((tm + 2 * POOL_HALO, width), F32)],
        compiler_params=_params("arbitrary"),
        name="pool_mix",
    )(ps, ps, ps, pool_w, pool_scale.reshape(1, width))


def _dft_tables(n):
    k = jnp.arange(n, dtype=I32)
    if n <= 1024:
        ang = ((k[:, None] * k[None, :]) % n).astype(F32) * (2.0 * np.pi / n)
        return jnp.cos(ang).astype(BF16), jnp.sin(ang).astype(BF16)
    lo = 64
    hi = n // lo
    t1 = jnp.arange(hi, dtype=I32)
    t0 = jnp.arange(lo, dtype=I32)
    a = ((k[:, None] * t1[None, :] * lo) % n).astype(F32) * (2.0 * np.pi / n)
    b = ((k[:, None] * t0[None, :]) % n).astype(F32) * (2.0 * np.pi / n)
    ca, sa, cb, sb = jnp.cos(a), jnp.sin(a), jnp.cos(b), jnp.sin(b)
    cos = ca[:, :, None] * cb[:, None, :] - sa[:, :, None] * sb[:, None, :]
    sin = sa[:, :, None] * cb[:, None, :] + ca[:, :, None] * sb[:, None, :]
    return cos.reshape(n, n).astype(BF16), sin.reshape(n, n).astype(BF16)


def _fnet_kernel(u_ref, cn_ref, sn_ref, cc_ref, sc_ref, o_ref, v1, v2, *, scale):
    @pl.when(pl.program_id(1) == 0)
    def _():
        gw = FNET_GROUP_W
        for g in range(FNET_GROUPS):
            lanes = slice(g * gw, (g + 1) * gw)
            ug = u_ref[:, lanes].astype(BF16)
            v1[:, lanes] = jnp.dot(ug, cc_ref[...], preferred_element_type=F32).astype(BF16)
            v2[:, lanes] = jnp.dot(ug, sc_ref[...], preferred_element_type=F32).astype(BF16)

    y = (jnp.dot(cn_ref[...], v1[...], preferred_element_type=F32)
         - jnp.dot(sn_ref[...], v2[...], preferred_element_type=F32))
    o_ref[...] = (y * scale).astype(o_ref.dtype)


def _fourier_mix(ps, cos_n, sin_n, cos_c, sin_c, n_seq, n):
    width = FNET_GROUPS * FNET_GROUP_W
    colb = COL_FNET * LANES // width
    tm = min(n, 512)
    nt = n // tm
    scale = float(1.0 / np.sqrt(n * FNET_GROUP_W))
    return pl.pallas_call(
        functools.partial(_fnet_kernel, scale=scale),
        grid=(n_seq, nt),
        in_specs=[pl.BlockSpec((n, width), lambda b, i: (b, colb)),
                  pl.BlockSpec((tm, n), lambda b, i: (i, 0)),
                  pl.BlockSpec((tm, n), lambda b, i: (i, 0)),
                  pl.BlockSpec((FNET_GROUP_W, FNET_GROUP_W), lambda b, i: (0, 0)),
                  pl.BlockSpec((FNET_GROUP_W, FNET_GROUP_W), lambda b, i: (0, 0))],
        out_specs=pl.BlockSpec((tm, width), lambda b, i: (b * nt + i, 0)),
        out_shape=jax.ShapeDtypeStruct((n_seq * n, width), BF16),
        scratch_shapes=[pltpu.VMEM((n, width), BF16), pltpu.VMEM((n, width), BF16)],
        compiler_params=_params("arbitrary", "arbitrary"),
        name="fourier_mix",
    )(ps, cos_n, sin_n, cos_c, sin_c)


MERGE_TILE = 256
MERGE_SUBTILES = 2


def _merge_kernel(x_ref, y0_ref, y1_ref, y2_ref, y3_ref, gl_ref, wb_ref, wo_ref, gain_ref, mod_ref,
                  wr_ref, x1_ref, h_ref, aff_ref):
    d = D_MODEL
    ts = x_ref.shape[0] // MERGE_SUBTILES
    for sub in range(MERGE_SUBTILES):
        rows = slice(sub * ts, (sub + 1) * ts)
        acc = None
        for i, y_ref in enumerate((y0_ref, y1_ref, y2_ref, y3_ref)):
            proj = jnp.dot(y_ref[rows, :], wb_ref[i], preferred_element_type=F32)
            gate = jax.nn.sigmoid(gl_ref[rows, i * d:(i + 1) * d].astype(F32))
            acc = gate * proj if acc is None else acc + gate * proj
        mix = jnp.dot(acc.astype(BF16), wo_ref[...], preferred_element_type=F32)
        x1 = x_ref[rows, :] + mod_ref[:, 2 * d:3 * d] * _rms(mix, gain_ref[1:2, :])
        x1_ref[rows, :] = x1
        h = _rms(x1, gain_ref[2:3, :]) * (1.0 + mod_ref[:, 4 * d:5 * d]) + mod_ref[:, 3 * d:4 * d]
        hbits = lax.bitcast_convert_type(h.astype(BF16).astype(F32), jnp.uint32)
        for s in range(H_SLABS):
            lo = hbits[:, (2 * s) * LANES:(2 * s + 1) * LANES] >> 16
            hi = hbits[:, (2 * s + 1) * LANES:(2 * s + 2) * LANES]
            h_ref[pl.ds(sub * ts * H_SLABS + s, ts, stride=H_SLABS), :] = hi | lo
        logits = jnp.dot(h, wr_ref[...], preferred_element_type=F32,
                         precision=lax.Precision.HIGHEST).T[:N_EXPERTS, :]
        logits = logits - jnp.max(logits, axis=0, keepdims=True)
        e = jnp.exp(logits)
        aff_ref[:, rows] = e / jnp.sum(e, axis=0, keepdims=True)


def _merge(x, ys, gl, wb, wo, gains, mod_rows, wr):
    n, d = x.shape
    tm = MERGE_TILE
    rows_per_mod = n // mod_rows.shape[0]
    const = dict(pipeline_mode=pl.Buffered(1))
    y_specs = [pl.BlockSpec((tm, BRANCH_W), functools.partial(lambda i, c: (i, c), c=cb))
               for _, cb in ys]
    return pl.pallas_call(
        _merge_kernel,
        grid=(n // tm,),
        in_specs=[pl.BlockSpec((tm, d), lambda i: (i, 0))] + y_specs + [
            pl.BlockSpec((tm, GATE_COLS), lambda i: (i, 0)),
            pl.BlockSpec((N_BRANCH, BRANCH_W, d), lambda i: (0, 0, 0), **const),
            pl.BlockSpec((d, d), lambda i: (0, 0), **const),
            pl.BlockSpec((4, d), lambda i: (0, 0)),
            pl.BlockSpec((None, 1, 6 * d), lambda i: ((i * tm) // rows_per_mod, 0, 0)),
            pl.BlockSpec((d, LANES), lambda i: (0, 0))],
        out_specs=[pl.BlockSpec((tm, d), lambda i: (i, 0)),
                   pl.BlockSpec((tm * H_SLABS, H_SLAB_W), lambda i: (i, 0)),
                   pl.BlockSpec((N_EXPERTS, tm), lambda i: (0, i))],
        out_shape=[jax.ShapeDtypeStruct((n, d), F32),
                   jax.ShapeDtypeStruct((n * H_SLABS, H_SLAB_W), jnp.uint32),
                   jax.ShapeDtypeStruct((N_EXPERTS, n), F32)],
        compiler_params=_params("arbitrary"),
        name="merge",
    )(x, *[a for a, _ in ys], gl, wb, wo, gains, mod_rows, wr)


CUMSUM_CHUNK = 256


def _lane_cumsum(x, upper):
    n = x.shape[1]
    carry = jnp.zeros((x.shape[0], 1), F32)
    out = []
    for c in range(n // CUMSUM_CHUNK):
        xc = x[:, c * CUMSUM_CHUNK:(c + 1) * CUMSUM_CHUNK]
        out.append(jnp.dot(xc.astype(BF16), upper, preferred_element_type=F32) + carry)
        carry = carry + jnp.sum(xc, axis=1, keepdims=True)
    return jnp.concatenate(out, axis=1) if len(out) > 1 else out[0]


def _route_kernel(a_ref, posm_ref, idx_ref, *, n, cap):
    bits = lax.bitcast_convert_type(a_ref[...], I32)
    thr = jnp.zeros((N_EXPERTS, 1), I32)
    for bit in range(30, -1, -1):
        cand = thr | (1 << bit)
        cnt = jnp.sum((bits >= cand).astype(F32), axis=1, keepdims=True)
        thr = jnp.where(cnt >= cap, cand, thr)
    gt = bits > thr
    eq = bits == thr
    need = cap - jnp.sum(gt.astype(F32), axis=1, keepdims=True)
    r = lax.broadcasted_iota(I32, (CUMSUM_CHUNK, CUMSUM_CHUNK), 0)
    c = lax.broadcasted_iota(I32, (CUMSUM_CHUNK, CUMSUM_CHUNK), 1)
    upper = (r <= c).astype(BF16)
    cum_eq = _lane_cumsum(eq.astype(F32), upper)
    sel = gt | (eq & (cum_eq <= need))
    pos = _lane_cumsum(sel.astype(F32), upper) - 1.0
    posm = jnp.where(sel, pos, -1.0).astype(I32)
    posm_ref[...] = posm
    tok = lax.broadcasted_iota(I32, (n, LANES), 0)
    lane = lax.broadcasted_iota(I32, (n, LANES), 1)
    tmat = jnp.where(lane == 0, tok // 64, jnp.where(lane == 1, tok % 64, 0)).astype(BF16)
    slot = lax.broadcasted_iota(I32, (cap, n), 0)
    for e in range(N_EXPERTS):
        onehot = (posm[e:e + 1, :] == slot).astype(BF16)
        idx_ref[e] = jnp.dot(onehot, tmat, preferred_element_type=F32)


def _route(aff_t, n_seq, n):
    cap = EC_CAPACITY_FACTOR * n // N_EXPERTS
    return pl.pallas_call(
        functools.partial(_route_kernel, n=n, cap=cap),
        grid=(n_seq,),
        in_specs=[pl.BlockSpec((N_EXPERTS, n), lambda b: (0, b))],
        out_specs=[pl.BlockSpec((N_EXPERTS, n), lambda b: (0, b)),
                   pl.BlockSpec((None, N_EXPERTS, cap, LANES), lambda b: (b, 0, 0, 0))],
        out_shape=[jax.ShapeDtypeStruct((N_EXPERTS, n_seq * n), I32),
                   jax.ShapeDtypeStruct((n_seq, N_EXPERTS, cap, LANES), F32)],
        compiler_params=_params("arbitrary"),
        name="route",
    )(aff_t)


COMBINE_TILE = 128
COMBINE_WIN = COMBINE_TILE + BF16_ROWS


def _expert_kernel(idx_ref, hc_ref, hl_ref, wg_ref, wu_ref, wd_ref, o_ref, xg, xb, hmid, sem,
                   *, mc, ml, nf):
    e = pl.program_id(0)
    f = pl.program_id(1)
    n_exp = pl.num_programs(0)
    m = mc + ml
    cc, cl = mc // nf, ml // nf

    def start_row(src_ref, expert, p):
        src = pl.multiple_of(idx_ref[expert * m + p] * H_SLABS, H_SLABS)
        dst = pl.multiple_of(p * H_SLABS, H_SLABS)
        pltpu.make_async_copy(src_ref.at[pl.ds(src, H_SLABS), :],
                              xg.at[pl.ds(dst, H_SLABS), :], sem).start()

    def wait_rows():
        pltpu.make_async_copy(hl_ref.at[pl.ds(0, m * H_SLABS), :], xg, sem).wait()

    @pl.when(f == 0)
    def _():
        @pl.when(e == 0)
        def _():
            def issue_ctx(p, carry):
                start_row(hc_ref, 0, p)
                return carry

            def issue_lat(p, carry):
                start_row(hl_ref, 0, mc + p)
                return carry

            lax.fori_loop(0, mc, issue_ctx, 0, unroll=8)
            lax.fori_loop(0, ml, issue_lat, 0, unroll=8)

        wait_rows()
        for s in range(H_SLABS):
            u = xg[pl.ds(s, m, stride=H_SLABS), :]
            even = lax.bitcast_convert_type(u << 16, F32)
            odd = lax.bitcast_convert_type(u & jnp.uint32(0xFFFF0000), F32)
            xb[:, (2 * s) * LANES:(2 * s + 1) * LANES] = even.astype(BF16)
            xb[:, (2 * s + 1) * LANES:(2 * s + 2) * LANES] = odd.astype(BF16)

    @pl.when(f < nf)
    def _():
        nxt = jnp.where(e + 1 < n_exp, e + 1, 0)
        for k in range(cc):
            start_row(hc_ref, nxt, f * cc + k)
        for k in range(cl):
            start_row(hl_ref, nxt, mc + f * cl + k)
        x = xb[...]
        a = jnp.dot(x, wg_ref[...].astype(BF16), preferred_element_type=F32)
        b = jnp.dot(x, wu_ref[...].astype(BF16), preferred_element_type=F32)
        hmid[f] = (a * jax.nn.sigmoid(a) * b).astype(BF16)

    @pl.when(f >= nf)
    def _():
        tf = hmid.shape[2]
        out = None
        for k in range(nf):
            part = jnp.dot(hmid[k], wd_ref[k * tf:(k + 1) * tf, :].astype(BF16),
                           preferred_element_type=F32)
            out = part if out is None else out + part
        o_ref[0:m, :] = out.astype(o_ref.dtype)
        o_ref[m:, :] = jnp.zeros((o_ref.shape[0] - m, o_ref.shape[1]), o_ref.dtype)

    @pl.when((e == n_exp - 1) & (f == pl.num_programs(1) - 1))
    def _():
        wait_rows()


def _expert_ffn(idx_flat, h_ctx, h_lat, w_gate, w_up, w_down, layer, mc, ml, tf, tn):
    _, n_exp, d, ff = w_gate.shape
    m = mc + ml
    m_pad = m + COMBINE_WIN
    nf = ff // tf
    assert mc % nf == 0 and ml % nf == 0 and h_lat.shape[0] >= m * H_SLABS
    grid_spec = pltpu.PrefetchScalarGridSpec(
        num_scalar_prefetch=1,
        grid=(n_exp, nf + d // tn),
        in_specs=[pl.BlockSpec(memory_space=pl.ANY),
                  pl.BlockSpec(memory_space=pl.ANY),
                  pl.BlockSpec((None, None, d, tf),
                               lambda e, f, idx: (layer, e, 0, jnp.minimum(f, nf - 1))),
                  pl.BlockSpec((None, None, d, tf),
                               lambda e, f, idx: (layer, e, 0, jnp.minimum(f, nf - 1))),
                  pl.BlockSpec((None, None, ff, tn),
                               lambda e, f, idx: (layer, e, 0, jnp.maximum(f - nf, 0)))],
        out_specs=pl.BlockSpec((None, m_pad, tn), lambda e, f, idx: (e, 0, jnp.maximum(f - nf, 0))),
        scratch_shapes=[pltpu.VMEM((m * H_SLABS, H_SLAB_W), jnp.uint32),
                        pltpu.VMEM((m, d), BF16),
                        pltpu.VMEM((nf, m, tf), BF16),
                        pltpu.SemaphoreType.DMA(())])
    return pl.pallas_call(
        functools.partial(_expert_kernel, mc=mc, ml=ml, nf=nf),
        grid_spec=grid_spec,
        out_shape=jax.ShapeDtypeStruct((n_exp, m_pad, d), BF16),
        compiler_params=_params("arbitrary", "arbitrary"),
        name="expert_ffn",
    )(idx_flat, h_ctx, h_lat, w_gate, w_up, w_down)


def _combine_kernel(w0_ref, x_ref, posm_ref, aff_ref, gain_ref, mod_ref, *rest):
    win_refs = rest[:N_EXPERTS]
    o_ref = rest[N_EXPERTS]
    t = pl.program_id(0)
    d = D_MODEL
    tt = COMBINE_TILE
    pad = jnp.zeros((LANES - 2 * N_EXPERTS, tt), F32)
    both = jnp.concatenate([posm_ref[...].astype(F32), aff_ref[...], pad], axis=0).T
    slot = lax.broadcasted_iota(I32, (tt, COMBINE_WIN), 1).astype(F32)
    ff = None
    for e in range(N_EXPERTS):
        rel = both[:, e:e + 1] - w0_ref[t * N_EXPERTS + e].astype(F32)
        weight = jnp.where(rel == slot, both[:, N_EXPERTS + e:N_EXPERTS + e + 1], 0.0).astype(BF16)
        c = jnp.dot(weight, win_refs[e][...], preferred_element_type=F32)
        ff = c if ff is None else ff + c
    o_ref[...] = x_ref[...] + mod_ref[:, 5 * d:6 * d] * _rms(ff, gain_ref[3:4, :])


def _combine(w0_flat, x1, posm, aff_t, gains, mod_rows, ys):
    n, d = x1.shape
    tt = COMBINE_TILE
    rows_per_mod = n // mod_rows.shape[0]

    def win_spec(e):
        return pl.BlockSpec(
            (None, pl.Element(COMBINE_WIN), pl.Element(d)),
            lambda t, w0: (e, pl.multiple_of(w0[t * N_EXPERTS + e], BF16_ROWS), 0))

    grid_spec = pltpu.PrefetchScalarGridSpec(
        num_scalar_prefetch=1,
        grid=(n // tt,),
        in_specs=[pl.BlockSpec((tt, d), lambda t, w0: (t, 0)),
                  pl.BlockSpec((N_EXPERTS, tt), lambda t, w0: (0, t)),
                  pl.BlockSpec((N_EXPERTS, tt), lambda t, w0: (0, t)),
                  pl.BlockSpec((4, d), lambda t, w0: (0, 0)),
                  pl.BlockSpec((None, 1, 6 * d), lambda t, w0: ((t * tt) // rows_per_mod, 0, 0))]
                 + [win_spec(e) for e in range(N_EXPERTS)],
        out_specs=pl.BlockSpec((tt, d), lambda t, w0: (t, 0)))
    return pl.pallas_call(
        _combine_kernel,
        grid_spec=grid_spec,
        out_shape=jax.ShapeDtypeStruct((n, d), F32),
        compiler_params=_params("arbitrary"),
        name="combine",
    )(w0_flat, x1, posm, aff_t, gains, mod_rows, *([ys] * N_EXPERTS))


def _slot_indices(idx_parts, n_seq, n):
    tok = (idx_parts[..., 0] * 64.0 + idx_parts[..., 1]).astype(I32)
    tok = tok + (jnp.arange(n_seq, dtype=I32) * n)[:, None, None]
    return jnp.transpose(tok, (1, 0, 2)).reshape(N_EXPERTS, -1)


def _window_starts(posm, n_seq, n, base):
    cap = EC_CAPACITY_FACTOR * n // N_EXPERTS
    tt = COMBINE_TILE
    sel = (posm >= 0).reshape(N_EXPERTS, n_seq, n // tt, tt)
    cnt = jnp.sum(sel.astype(I32), axis=-1)
    before = jnp.cumsum(cnt, axis=-1) - cnt
    start = before + (jnp.arange(n_seq, dtype=I32) * cap)[None, :, None] + base
    start = (start // BF16_ROWS) * BF16_ROWS
    return jnp.transpose(start.reshape(N_EXPERTS, -1), (1, 0)).reshape(-1)


def _global_positions(posm, n_seq, n, base):
    cap = EC_CAPACITY_FACTOR * n // N_EXPERTS
    offs = jnp.repeat(jnp.arange(n_seq, dtype=I32) * cap, n)[None, :] + base
    return jnp.where(posm >= 0, posm + offs, -1)


FFN_HIDDEN_TILE = 256
FFN_OUT_TILE = 256


def _channel_sublayers(ctx, lat, gains, w_gate, w_up, w_down, layer):
    routed = []
    for (x1, h, aff_t, mod_rows, n_seq, n) in (ctx, lat):
        posm, idx_parts = _route(aff_t, n_seq, n)
        routed.append((posm, _slot_indices(idx_parts, n_seq, n)))
    mc, ml = routed[0][1].shape[1], routed[1][1].shape[1]
    idx_flat = jnp.concatenate([routed[0][1], routed[1][1]], axis=1).reshape(-1)
    ys = _expert_ffn(idx_flat, ctx[1], lat[1], w_gate, w_up, w_down, layer, mc, ml,
                     FFN_HIDDEN_TILE, FFN_OUT_TILE)
    outs = []
    for (x1, h, aff_t, mod_rows, n_seq, n), (posm, _), base in zip((ctx, lat), routed, (0, mc)):
        outs.append(_combine(_window_starts(posm, n_seq, n, base), x1,
                             _global_positions(posm, n_seq, n, base), aff_t, gains, mod_rows, ys))
    return outs


def kernel(x_prompt, x_sample, c, cache_na_k, cache_na_v, cache_win_k, cache_win_v, c_ctx, w_ada, b_ada, norm_gain, w_in, na_rpb, win_sink, pool_w, pool_scale, w_branch, w_out, w_router, w_e_gate, w_e_up, w_e_down):
    batch, seq, d = x_prompt.shape
    dec_batch, dec_seq, _ = x_sample.shape
    depth = w_in.shape[0]
    past = cache_na_k.shape[2]
    assert d == D_MODEL and dec_seq % GRID_W == 0 and dec_batch <= SUBLANES - 1

    cvecs = jnp.zeros((SUBLANES, d), F32).at[0].set(c_ctx).at[1:1 + dec_batch].set(c)
    mods = _modulation(cvecs, w_ada, b_ada)

    cos_r, sin_r = _rope_tables(dec_seq)
    cos_ctx, sin_ctx = _dft_tables(seq)
    cos_lat, sin_lat = _dft_tables(dec_seq)
    cos_c, sin_c = _dft_tables(FNET_GROUP_W)

    xp = x_prompt.reshape(batch * seq, d)
    xs = x_sample.reshape(dec_batch * dec_seq, d)
    na_k, na_v, win_k, win_v = [], [], [], []
    for l in range(depth):
        gains = norm_gain[l]
        wb = w_branch[l].astype(BF16)
        wo = w_out[l].astype(BF16)
        wr = jnp.pad(w_router[l], ((0, 0), (0, LANES - N_EXPERTS)))
        mod_ctx = mods[l, 0:1].reshape(1, 1, 6 * d)
        mod_lat = mods[l, 1:1 + dec_batch].reshape(dec_batch, 1, 6 * d)

        ps, gl = _in_projection(xp, gains[0], mod_ctx, w_in, l)
        y_attn = _ctx_attention(ps, win_sink[l], batch, seq)
        y_pool = _pool_mix(ps, pool_w[l], pool_scale[l], batch, seq)
        y_fnet = _fourier_mix(ps, cos_ctx, sin_ctx, cos_c, sin_c, batch, seq)
        merged_ctx = _merge(xp, [(y_attn, 0), (y_pool, 0), (y_attn, 1), (y_fnet, 0)],
                            gl, wb, wo, gains, mod_ctx, wr)
        hd = HEAD_DIM
        na_k.append(ps[:, COL_NA_K * hd:(COL_NA_K + NA_HEADS) * hd])
        na_v.append(ps[:, COL_NA_V * hd:(COL_NA_V + NA_HEADS) * hd])
        win_k.append(ps[:, COL_WIN_K * hd:(COL_WIN_K + WIN_KV_HEADS) * hd])
        win_v.append(ps[:, COL_WIN_V * hd:(COL_WIN_V + WIN_KV_HEADS) * hd])

        ps, gl = _in_projection(xs, gains[0], mod_lat, w_in, l)
        table = _na_bias_table(na_rpb[l], dec_seq // GRID_W)
        y_na = _latent_na(ps, cache_na_k[:, l].reshape(dec_batch, past, NA_HEADS * hd),
                          cache_na_v[:, l].reshape(dec_batch, past, NA_HEADS * hd),
                          table, dec_batch, dec_seq)
        y_win = _latent_win(ps, cache_win_k[:, l].reshape(dec_batch, past, WIN_KV_HEADS * hd),
                            cache_win_v[:, l].reshape(dec_batch, past, WIN_KV_HEADS * hd),
                            win_sink[l], cos_r, sin_r, dec_batch, dec_seq)
        y_pool = _pool_mix(ps, pool_w[l], pool_scale[l], dec_batch, dec_seq)
        y_fnet = _fourier_mix(ps, cos_lat, sin_lat, cos_c, sin_c, dec_batch, dec_seq)
        merged_lat = _merge(xs, [(y_na, 0), (y_pool, 0), (y_win, 0), (y_fnet, 0)],
                            gl, wb, wo, gains, mod_lat, wr)

        xp, xs = _channel_sublayers((*merged_ctx, mod_ctx, batch, seq),
                                    (*merged_lat, mod_lat, dec_batch, dec_seq),
                                    gains, w_e_gate, w_e_up, w_e_down, l)

    def stack(parts, heads):
        return jnp.stack([p.reshape(batch, seq, heads, HEAD_DIM) for p in parts], axis=1)

    return (xp.reshape(batch, seq, d), xs.reshape(dec_batch, dec_seq, d),
            stack(na_k, NA_HEADS), stack(na_v, NA_HEADS),
            stack(win_k, WIN_KV_HEADS), stack(win_v, WIN_KV_HEADS))
```

```python
import functools

import numpy as np
import jax
import jax.numpy as jnp
from jax import lax
from jax.experimental import pallas as pl
from jax.experimental.pallas import tpu as pltpu

F32 = jnp.float32
BF16 = jnp.bfloat16
I32 = jnp.int32

D_MODEL = 2048
GRID_W = 64
HEAD_DIM = 128
N_BRANCH = 4
BRANCH_W = 512
NA_HEADS = 4
NA_MAX_ROWS = 8
NA_COLS = 16
WIN_HEADS = 4
WIN_KV_HEADS = 2
WIN_RADIUS = 128
WIN_BLOCK = 128
POOL_GROUPS = 4
POOL_GROUP_W = 128
POOL_WINDOWS = (2, 4, 8, 16)
FNET_GROUPS = 4
FNET_GROUP_W = 128
N_EXPERTS = 16
EC_CAPACITY_FACTOR = 2
ROPE_THETA = 10000.0
NORM_EPS = 1e-6
NEG_INF = -1e30
ATTN_SCALE = HEAD_DIM ** -0.5

SMALL_COLS = 3584
COL_NA_Q, COL_NA_K, COL_NA_V = 0, 4, 8
COL_POOL = 12
COL_WIN_Q, COL_WIN_K, COL_WIN_V = 16, 20, 22
COL_FNET = 24
GATE_COLS = N_BRANCH * D_MODEL

VMEM_LIMIT_BYTES = 60 * 1024 * 1024
LANES = 128
SUBLANES = 8
BF16_ROWS = 16

H_SLAB_W = LANES
H_SLABS = D_MODEL // (2 * LANES)


def _params(*sem):
    return pltpu.CompilerParams(dimension_semantics=sem, vmem_limit_bytes=VMEM_LIMIT_BYTES)


def _dot(a, b):
    return jnp.dot(a.astype(BF16), b.astype(BF16), preferred_element_type=F32)


def _dot_nt(a, b):
    return lax.dot_general(a.astype(BF16), b.astype(BF16), (((1,), (1,)), ((), ())),
                           preferred_element_type=F32)


def _rms(x, gain):
    return x * lax.rsqrt(jnp.mean(x * x, axis=-1, keepdims=True) + NORM_EPS) * gain


def _mod_kernel(c_ref, w_ref, b_ref, o_ref):
    c = c_ref[...]
    s = c * jax.nn.sigmoid(c)
    o_ref[...] = jnp.dot(s, w_ref[...], preferred_element_type=F32,
                         precision=lax.Precision.HIGHEST) + b_ref[...]


def _modulation(cvecs, w_ada, b_ada):
    depth, d, n6 = w_ada.shape
    tn = 1024
    return pl.pallas_call(
        _mod_kernel,
        grid=(depth, n6 // tn),
        in_specs=[pl.BlockSpec((SUBLANES, d), lambda l, j: (0, 0)),
                  pl.BlockSpec((None, d, tn), lambda l, j: (l, 0, j)),
                  pl.BlockSpec((None, 1, tn), lambda l, j: (l, 0, j))],
        out_specs=pl.BlockSpec((None, SUBLANES, tn), lambda l, j: (l, 0, j)),
        out_shape=jax.ShapeDtypeStruct((depth, SUBLANES, n6), F32),
        compiler_params=_params("arbitrary", "arbitrary"),
        name="modulation",
    )(cvecs, w_ada, b_ada.reshape(depth, 1, n6))


def _inproj_kernel(x_ref, g_ref, mod_ref, w_ref, ps_ref, gl_ref, h_scr, *, n_small):
    j = pl.program_id(1)

    @pl.when(j == 0)
    def _():
        d = D_MODEL
        h = _rms(x_ref[...], g_ref[...]) * (1.0 + mod_ref[:, d:2 * d]) + mod_ref[:, 0:d]
        h_scr[...] = h.astype(BF16)

    r = jnp.dot(h_scr[...], w_ref[...].astype(BF16), preferred_element_type=F32)

    @pl.when(j < n_small)
    def _():
        ps_ref[...] = r

    @pl.when(j >= n_small)
    def _():
        gl_ref[...] = r.astype(BF16)


def _in_projection(x, gain, mod_rows, w_in, layer):
    n, d = x.shape
    tm, tn = 1024, 512
    n_small = SMALL_COLS // tn
    n_tiles = w_in.shape[2] // tn
    rows_per_mod = n // mod_rows.shape[0]
    return pl.pallas_call(
        functools.partial(_inproj_kernel, n_small=n_small),
        grid=(n // tm, n_tiles),
        in_specs=[pl.BlockSpec((tm, d), lambda i, j: (i, 0)),
                  pl.BlockSpec((1, d), lambda i, j: (0, 0)),
                  pl.BlockSpec((None, 1, 6 * d), lambda i, j: ((i * tm) // rows_per_mod, 0, 0)),
                  pl.BlockSpec((None, d, tn), lambda i, j: (layer, 0, j))],
        out_specs=[pl.BlockSpec((tm, tn), lambda i, j: (i, jnp.minimum(j, n_small - 1))),
                   pl.BlockSpec((tm, tn), lambda i, j: (i, jnp.maximum(j - n_small, 0)))],
        out_shape=[jax.ShapeDtypeStruct((n, SMALL_COLS), F32),
                   jax.ShapeDtypeStruct((n, GATE_COLS), BF16)],
        scratch_shapes=[pltpu.VMEM((tm, d), BF16)],
        compiler_params=_params("arbitrary", "arbitrary"),
        name="in_projection",
    )(x, gain.reshape(1, d), mod_rows, w_in)


def _full_attention(q, k, v, sink):
    s = _dot_nt(q, k) * ATTN_SCALE
    m = jnp.max(s, axis=-1, keepdims=True)
    if sink is not None:
        m = jnp.maximum(m, sink)
    p = jnp.exp(s - m)
    den = jnp.sum(p, axis=-1, keepdims=True)
    if sink is not None:
        den = den + jnp.exp(sink - m)
    return _dot(p, v) / den


def _ctx_attn_kernel(sink_ref, qa_ref, ka_ref, va_ref, qw_ref, kw_ref, vw_ref, o_ref):
    hd = HEAD_DIM
    group = WIN_HEADS // WIN_KV_HEADS
    for h in range(NA_HEADS):
        sl = slice(h * hd, (h + 1) * hd)
        o_ref[:, sl] = _full_attention(qa_ref[:, sl], ka_ref[:, sl], va_ref[:, sl],
                                       None).astype(o_ref.dtype)
    for h in range(WIN_HEADS):
        sl = slice(h * hd, (h + 1) * hd)
        kv = slice((h // group) * hd, (h // group + 1) * hd)
        o_ref[:, (NA_HEADS + h) * hd:(NA_HEADS + h + 1) * hd] = _full_attention(
            qw_ref[:, sl], kw_ref[:, kv], vw_ref[:, kv], sink_ref[h]).astype(o_ref.dtype)


def _ctx_attention(ps, sinks, batch, seq):
    n_heads = NA_HEADS + WIN_HEADS
    na_w = NA_HEADS * HEAD_DIM
    kv_w = WIN_KV_HEADS * HEAD_DIM

    def cols(width, first_lane_block):
        return pl.BlockSpec((seq, width), lambda b: (b, first_lane_block * LANES // width))

    return pl.pallas_call(
        _ctx_attn_kernel,
        grid=(batch,),
        in_specs=[pl.BlockSpec(memory_space=pltpu.SMEM),
                  cols(na_w, COL_NA_Q), cols(na_w, COL_NA_K), cols(na_w, COL_NA_V),
                  cols(na_w, COL_WIN_Q), cols(kv_w, COL_WIN_K), cols(kv_w, COL_WIN_V)],
        out_specs=pl.BlockSpec((seq, n_heads * HEAD_DIM), lambda b: (b, 0)),
        out_shape=jax.ShapeDtypeStruct((batch * seq, n_heads * HEAD_DIM), BF16),
        compiler_params=_params("arbitrary"),
        name="ctx_attention",
    )(sinks, ps, ps, ps, ps, ps, ps)


NA_QROWS = 4
NA_QBLK = NA_QROWS * GRID_W
NA_KBLK = 3 * NA_QBLK


def _na_bias_table(rpb, rows):
    n_blocks = rows // NA_QROWS
    jrep = np.array([0, 1, n_blocks - 1])
    rr = np.arange(NA_QROWS)
    qc = np.arange(GRID_W)
    kk = np.arange(3 * NA_QROWS)
    kc = np.arange(GRID_W)
    r = NA_QROWS * jrep[:, None] + rr[None, :]
    krow = NA_QROWS * (jrep[:, None] - 1) + kk[None, :]
    row_start = np.clip(r - NA_MAX_ROWS // 2, 0, rows - NA_MAX_ROWS)
    kr3 = krow[:, None, :]
    vrow = ((kr3 >= row_start[:, :, None]) & (kr3 < row_start[:, :, None] + NA_MAX_ROWS)
            & (kr3 >= 0) & (kr3 < rows))
    dr = np.clip(kr3 - r[:, :, None] + NA_MAX_ROWS - 1, 0, 2 * NA_MAX_ROWS - 2)
    ws = np.clip(qc - NA_COLS // 2, 0, GRID_W - NA_COLS)
    vcol = (kc[None, :] >= ws[:, None]) & (kc[None, :] < ws[:, None] + NA_COLS)
    dc = np.clip(kc[None, :] - qc[:, None] + NA_COLS - 1, 0, 2 * NA_COLS - 2)
    n_dr, n_dc = 2 * NA_MAX_ROWS - 1, 2 * NA_COLS - 1
    sel_r = (dr.reshape(-1)[:, None] == np.arange(n_dr)[None, :]).astype(np.float32)
    sel_c = (np.arange(n_dc)[:, None] == dc.reshape(-1)[None, :]).astype(np.float32)
    rows_sel = jnp.einsum("rd,hdc->hrc", sel_r, rpb.astype(F32), precision=lax.Precision.HIGHEST)
    bias = jnp.einsum("hrc,cq->hrq", rows_sel, sel_c, precision=lax.Precision.HIGHEST)
    bias = bias.reshape(rpb.shape[0], 3, NA_QROWS, 3 * NA_QROWS, GRID_W, GRID_W)
    bias = jnp.transpose(bias, (0, 1, 2, 4, 3, 5))
    valid = vrow[:, :, None, :, None] & vcol[None, None, :, None, :]
    tbl = jnp.where(valid[None], bias, NEG_INF)
    return tbl.reshape(rpb.shape[0], 3, NA_QBLK, NA_KBLK)


def _na_kernel(q_ref, kp_ref, kc_ref, kn_ref, vp_ref, vc_ref, vn_ref, kx_ref, vx_ref, t_ref, o_ref):
    for h in range(NA_HEADS):
        sl = slice(h * HEAD_DIM, (h + 1) * HEAD_DIM)
        q = q_ref[:, sl].astype(BF16)
        k = jnp.concatenate([kp_ref[:, sl], kc_ref[:, sl], kn_ref[:, sl]], axis=0)
        v = jnp.concatenate([vp_ref[:, sl], vc_ref[:, sl], vn_ref[:, sl]], axis=0)
        s_loc = _dot_nt(q, k) * ATTN_SCALE + t_ref[h]
        s_ctx = _dot_nt(q, kx_ref[:, sl]) * ATTN_SCALE
        m = jnp.maximum(jnp.max(s_loc, axis=-1, keepdims=True),
                        jnp.max(s_ctx, axis=-1, keepdims=True))
        p_loc = jnp.exp(s_loc - m)
        p_ctx = jnp.exp(s_ctx - m)
        den = jnp.sum(p_loc, axis=-1, keepdims=True) + jnp.sum(p_ctx, axis=-1, keepdims=True)
        o = _dot(p_loc, v) + _dot(p_ctx, vx_ref[:, sl])
        o_ref[:, sl] = (o / den).astype(o_ref.dtype)


def _latent_na(ps, cache_k, cache_v, table, batch, n):
    nb = n // NA_QBLK
    past = cache_k.shape[1]
    width = NA_HEADS * HEAD_DIM
    blk = (NA_QBLK, width)

    def cur(col):
        return pl.BlockSpec(blk, lambda b, j: (b * nb + j, col * LANES // width))

    def prev(col):
        return pl.BlockSpec(blk, lambda b, j: (b * nb + jnp.maximum(j - 1, 0), col * LANES // width))

    def nxt(col):
        return pl.BlockSpec(blk, lambda b, j: (b * nb + jnp.minimum(j + 1, nb - 1),
                                               col * LANES // width))

    cache_spec = pl.BlockSpec((None, past, width), lambda b, j: (b, 0, 0))
    table_spec = pl.BlockSpec(
        (NA_HEADS, None, NA_QBLK, NA_KBLK),
        lambda b, j: (0, jnp.where(j == 0, 0, jnp.where(j == nb - 1, 2, 1)), 0, 0))
    return pl.pallas_call(
        _na_kernel,
        grid=(batch, nb),
        in_specs=[cur(COL_NA_Q), prev(COL_NA_K), cur(COL_NA_K), nxt(COL_NA_K),
                  prev(COL_NA_V), cur(COL_NA_V), nxt(COL_NA_V),
                  cache_spec, cache_spec, table_spec],
        out_specs=pl.BlockSpec(blk, lambda b, j: (b * nb + j, 0)),
        out_shape=jax.ShapeDtypeStruct((batch * n, width), BF16),
        compiler_params=_params("arbitrary", "arbitrary"),
        name="latent_na",
    )(ps, ps, ps, ps, ps, ps, ps, cache_k, cache_v, table)


def _rope_tables(n):
    t = jnp.arange(n)
    row = (t // GRID_W).astype(F32)
    col = (t % GRID_W).astype(F32)
    n_freq = HEAD_DIM // 4
    inv = ROPE_THETA ** (-jnp.arange(n_freq, dtype=F32) / n_freq)
    ang = jnp.concatenate([row[:, None] * inv, col[:, None] * inv], axis=-1)
    cos = jnp.repeat(jnp.cos(ang), 2, axis=-1)
    sin = jnp.repeat(jnp.sin(ang), 2, axis=-1)
    sign = jnp.tile(jnp.array([-1.0, 1.0], F32), HEAD_DIM // 2)
    return cos, sin * sign


def _rope(x, cos, sin_signed):
    lane = lax.broadcasted_iota(I32, x.shape, 1)
    partner = jnp.where(lane % 2 == 0, pltpu.roll(x, HEAD_DIM - 1, 1), pltpu.roll(x, 1, 1))
    return x * cos + partner * sin_signed


def _win_kernel(sink_ref, q_ref, kp_ref, kc_ref, kn_ref, vp_ref, vc_ref, vn_ref, kx_ref, vx_ref,
                cq_ref, sq_ref, cp_ref, sp_ref, cc_ref, sc_ref, cn_ref, sn_ref, o_ref, *, nb):
    j = pl.program_id(1)
    bs = WIN_BLOCK
    hd = HEAD_DIM
    group = WIN_HEADS // WIN_KV_HEADS
    cq = jnp.concatenate([cq_ref[...]] * group, axis=0)
    sq = jnp.concatenate([sq_ref[...]] * group, axis=0)
    qq = lax.broadcasted_iota(I32, (group * bs, 3 * bs), 0) % bs
    kk = lax.broadcasted_iota(I32, (group * bs, 3 * bs), 1)
    valid = jnp.abs(kk - bs - qq) <= WIN_RADIUS
    valid = valid & ((kk >= bs) | (j > 0)) & ((kk < 2 * bs) | (j < nb - 1))
    rows = lax.broadcasted_iota(I32, (group * bs, 1), 0)
    for kv in range(WIN_KV_HEADS):
        sl = slice(kv * hd, (kv + 1) * hd)
        q = jnp.concatenate([q_ref[:, (kv * group + g) * hd:(kv * group + g + 1) * hd]
                             for g in range(group)], axis=0)
        q_rot = _rope(q, cq, sq)
        k = jnp.concatenate([_rope(kp_ref[:, sl], cp_ref[...], sp_ref[...]),
                             _rope(kc_ref[:, sl], cc_ref[...], sc_ref[...]),
                             _rope(kn_ref[:, sl], cn_ref[...], sn_ref[...])], axis=0)
        v = jnp.concatenate([vp_ref[:, sl], vc_ref[:, sl], vn_ref[:, sl]], axis=0)
        s_loc = jnp.where(valid, _dot_nt(q_rot, k) * ATTN_SCALE, NEG_INF)
        s_ctx = _dot_nt(q, kx_ref[:, sl]) * ATTN_SCALE
        sk = jnp.zeros((group * bs, 1), F32)
        for g in range(group):
            sk = jnp.where(rows // bs == g, sink_ref[kv * group + g], sk)
        m = jnp.maximum(jnp.maximum(jnp.max(s_loc, axis=-1, keepdims=True),
                                    jnp.max(s_ctx, axis=-1, keepdims=True)), sk)
        p_loc = jnp.exp(s_loc - m)
        p_ctx = jnp.exp(s_ctx - m)
        den = (jnp.sum(p_loc, axis=-1, keepdims=True) + jnp.sum(p_ctx, axis=-1, keepdims=True)
               + jnp.exp(sk - m))
        o = (_dot(p_loc, v) + _dot(p_ctx, vx_ref[:, sl])) / den
        for g in range(group):
            o_ref[:, (kv * group + g) * hd:(kv * group + g + 1) * hd] = (
                o[g * bs:(g + 1) * bs].astype(o_ref.dtype))


def _latent_win(ps, cache_k, cache_v, sinks, cos, sin, batch, n):
    nb = n // WIN_BLOCK
    past = cache_k.shape[1]
    q_w = WIN_HEADS * HEAD_DIM
    kv_w = WIN_KV_HEADS * HEAD_DIM

    def cur_idx(j):
        return j

    def prev_idx(j):
        return jnp.maximum(j - 1, 0)

    def next_idx(j):
        return jnp.minimum(j + 1, nb - 1)

    def tok(col, fn):
        return pl.BlockSpec((WIN_BLOCK, kv_w), lambda b, j: (b * nb + fn(j), col * LANES // kv_w))

    def rope_spec(fn):
        return pl.BlockSpec((WIN_BLOCK, HEAD_DIM), lambda b, j: (fn(j), 0))

    q_spec = pl.BlockSpec((WIN_BLOCK, q_w), lambda b, j: (b * nb + j, COL_WIN_Q * LANES // q_w))
    cache_spec = pl.BlockSpec((None, past, kv_w), lambda b, j: (b, 0, 0))
    return pl.pallas_call(
        functools.partial(_win_kernel, nb=nb),
        grid=(batch, nb),
        in_specs=[pl.BlockSpec(memory_space=pltpu.SMEM), q_spec,
                  tok(COL_WIN_K, prev_idx), tok(COL_WIN_K, cur_idx), tok(COL_WIN_K, next_idx),
                  tok(COL_WIN_V, prev_idx), tok(COL_WIN_V, cur_idx), tok(COL_WIN_V, next_idx),
                  cache_spec, cache_spec,
                  rope_spec(cur_idx), rope_spec(cur_idx), rope_spec(prev_idx), rope_spec(prev_idx),
                  rope_spec(cur_idx), rope_spec(cur_idx), rope_spec(next_idx), rope_spec(next_idx)],
        out_specs=pl.BlockSpec((WIN_BLOCK, q_w), lambda b, j: (b * nb + j, 0)),
        out_shape=jax.ShapeDtypeStruct((batch * n, q_w), BF16),
        compiler_params=_params("arbitrary", "arbitrary"),
        name="latent_win",
    )(sinks, ps, ps, ps, ps, ps, ps, ps, cache_k, cache_v,
      cos, sin, cos, sin, cos, sin, cos, sin)


POOL_TILE = 256
POOL_HALO = SUBLANES


def _pool_kernel(up_ref, uc_ref, un_ref, w_ref, sc_ref, o_ref, ext, *, tiles_per_seq, n):
    lt = pl.program_id(0) % tiles_per_seq
    tm = POOL_TILE
    gw = POOL_GROUP_W
    ext[0:POOL_HALO, :] = jnp.where(lt > 0, up_ref[...], 0.0)
    ext[POOL_HALO:POOL_HALO + tm, :] = uc_ref[...]
    ext[POOL_HALO + tm:2 * POOL_HALO + tm, :] = jnp.where(lt < tiles_per_seq - 1, un_ref[...], 0.0)
    pos = lt * tm + lax.broadcasted_iota(I32, (tm, 1), 0)
    for g in range(POOL_GROUPS):
        half = POOL_WINDOWS[g] // 2
        lanes = slice(g * gw, (g + 1) * gw)
        acc = ext[pl.ds(POOL_HALO - half, tm), lanes]
        for dlt in range(-half + 1, half):
            acc = acc + ext[pl.ds(POOL_HALO + dlt, tm), lanes]
        cnt = (jnp.minimum(pos + half, n) - jnp.maximum(pos - half, 0)).astype(F32)
        pooled = acc / cnt - uc_ref[:, lanes]
        y = _dot(pooled, w_ref[g]) * sc_ref[:, lanes]
        o_ref[:, lanes] = y.astype(o_ref.dtype)


def _pool_mix(ps, pool_w, pool_scale, n_seq, n):
    tm = POOL_TILE
    tps = n // tm
    n_tiles = n_seq * tps
    halo_per_tile = tm // POOL_HALO
    width = POOL_GROUPS * POOL_GROUP_W
    colb = COL_POOL * LANES // width
    last_halo = n_tiles * halo_per_tile - 1
    return pl.pallas_call(
        functools.partial(_pool_kernel, tiles_per_seq=tps, n=n),
        grid=(n_tiles,),
        in_specs=[pl.BlockSpec((POOL_HALO, width),
                               lambda t: (jnp.maximum(t * halo_per_tile - 1, 0), colb)),
                  pl.BlockSpec((tm, width), lambda t: (t, colb)),
                  pl.BlockSpec((POOL_HALO, width),
                               lambda t: (jnp.minimum((t + 1) * halo_per_tile, last_halo), colb)),
                  pl.BlockSpec((POOL_GROUPS, POOL_GROUP_W, POOL_GROUP_W), lambda t: (0, 0, 0)),
                  pl.BlockSpec((1, width), lambda t: (0, 0))],
        out_specs=pl.BlockSpec((tm, width), lambda t: (t, 0)),
        out_shape=jax.ShapeDtypeStruct((n_seq * n, width), BF16),
        scratch_shapes=[pltpu.VMEM((tm + 2 * POOL_HALO, width), F32)],
        compiler_params=_params("arbitrary"),
        name="pool_mix",
    )(ps, ps, ps, pool_w, pool_scale.reshape(1, width))


def _dft_tables(n):
    k = jnp.arange(n, dtype=I32)
    if n <= 1024:
        ang = ((k[:, None] * k[None, :]) % n).astype(F32) * (2.0 * np.pi / n)
        return jnp.cos(ang).astype(BF16), jnp.sin(ang).astype(BF16)
    lo = 64
    hi = n // lo
    t1 = jnp.arange(hi, dtype=I32)
    t0 = jnp.arange(lo, dtype=I32)
    a = ((k[:, None] * t1[None, :] * lo) % n).astype(F32) * (2.0 * np.pi / n)
    b = ((k[:, None] * t0[None, :]) % n).astype(F32) * (2.0 * np.pi / n)
    ca, sa, cb, sb = jnp.cos(a), jnp.sin(a), jnp.cos(b), jnp.sin(b)
    cos = ca[:, :, None] * cb[:, None, :] - sa[:, :, None] * sb[:, None, :]
    sin = sa[:, :, None] * cb[:, None, :] + ca[:, :, None] * sb[:, None, :]
    return cos.reshape(n, n).astype(BF16), sin.reshape(n, n).astype(BF16)


def _fnet_kernel(u_ref, cn_ref, sn_ref, cc_ref, sc_ref, o_ref, v1, v2, *, scale):
    @pl.when(pl.program_id(1) == 0)
    def _():
        gw = FNET_GROUP_W
        for g in range(FNET_GROUPS):
            lanes = slice(g * gw, (g + 1) * gw)
            ug = u_ref[:, lanes].astype(BF16)
            v1[:, lanes] = jnp.dot(ug, cc_ref[...], preferred_element_type=F32).astype(BF16)
            v2[:, lanes] = jnp.dot(ug, sc_ref[...], preferred_element_type=F32).astype(BF16)

    y = (jnp.dot(cn_ref[...], v1[...], preferred_element_type=F32)
         - jnp.dot(sn_ref[...], v2[...], preferred_element_type=F32))
    o_ref[...] = (y * scale).astype(o_ref.dtype)


def _fourier_mix(ps, cos_n, sin_n, cos_c, sin_c, n_seq, n):
    width = FNET_GROUPS * FNET_GROUP_W
    colb = COL_FNET * LANES // width
    tm = min(n, 512)
    nt = n // tm
    scale = float(1.0 / np.sqrt(n * FNET_GROUP_W))
    return pl.pallas_call(
        functools.partial(_fnet_kernel, scale=scale),
        grid=(n_seq, nt),
        in_specs=[pl.BlockSpec((n, width), lambda b, i: (b, colb)),
                  pl.BlockSpec((tm, n), lambda b, i: (i, 0)),
                  pl.BlockSpec((tm, n), lambda b, i: (i, 0)),
                  pl.BlockSpec((FNET_GROUP_W, FNET_GROUP_W), lambda b, i: (0, 0)),
                  pl.BlockSpec((FNET_GROUP_W, FNET_GROUP_W), lambda b, i: (0, 0))],
        out_specs=pl.BlockSpec((tm, width), lambda b, i: (b * nt + i, 0)),
        out_shape=jax.ShapeDtypeStruct((n_seq * n, width), BF16),
        scratch_shapes=[pltpu.VMEM((n, width), BF16), pltpu.VMEM((n, width), BF16)],
        compiler_params=_params("arbitrary", "arbitrary"),
        name="fourier_mix",
    )(ps, cos_n, sin_n, cos_c, sin_c)


MERGE_TILE = 256
MERGE_SUBTILES = 2


def _merge_kernel(x_ref, y0_ref, y1_ref, y2_ref, y3_ref, gl_ref, wb_ref, wo_ref, gain_ref, mod_ref,
                  wr_ref, x1_ref, h_ref, aff_ref):
    d = D_MODEL
    ts = x_ref.shape[0] // MERGE_SUBTILES
    for sub in range(MERGE_SUBTILES):
        rows = slice(sub * ts, (sub + 1) * ts)
        acc = None
        for i, y_ref in enumerate((y0_ref, y1_ref, y2_ref, y3_ref)):
            proj = jnp.dot(y_ref[rows, :], wb_ref[i], preferred_element_type=F32)
            gate = jax.nn.sigmoid(gl_ref[rows, i * d:(i + 1) * d].astype(F32))
            acc = gate * proj if acc is None else acc + gate * proj
        mix = jnp.dot(acc.astype(BF16), wo_ref[...], preferred_element_type=F32)
        x1 = x_ref[rows, :] + mod_ref[:, 2 * d:3 * d] * _rms(mix, gain_ref[1:2, :])
        x1_ref[rows, :] = x1
        h = _rms(x1, gain_ref[2:3, :]) * (1.0 + mod_ref[:, 4 * d:5 * d]) + mod_ref[:, 3 * d:4 * d]
        hbits = lax.bitcast_convert_type(h.astype(BF16).astype(F32), jnp.uint32)
        for s in range(H_SLABS):
            lo = hbits[:, (2 * s) * LANES:(2 * s + 1) * LANES] >> 16
            hi = hbits[:, (2 * s + 1) * LANES:(2 * s + 2) * LANES]
            h_ref[pl.ds(sub * ts * H_SLABS + s, ts, stride=H_SLABS), :] = hi | lo
        logits = jnp.dot(h, wr_ref[...], preferred_element_type=F32,
                         precision=lax.Precision.HIGHEST).T[:N_EXPERTS, :]
        logits = logits - jnp.max(logits, axis=0, keepdims=True)
        e = jnp.exp(logits)
        aff_ref[:, rows] = e / jnp.sum(e, axis=0, keepdims=True)


def _merge(x, ys, gl, wb, wo, gains, mod_rows, wr):
    n, d = x.shape
    tm = MERGE_TILE
    rows_per_mod = n // mod_rows.shape[0]
    const = dict(pipeline_mode=pl.Buffered(1))
    y_specs = [pl.BlockSpec((tm, BRANCH_W), functools.partial(lambda i, c: (i, c), c=cb))
               for _, cb in ys]
    return pl.pallas_call(
        _merge_kernel,
        grid=(n // tm,),
        in_specs=[pl.BlockSpec((tm, d), lambda i: (i, 0))] + y_specs + [
            pl.BlockSpec((tm, GATE_COLS), lambda i: (i, 0)),
            pl.BlockSpec((N_BRANCH, BRANCH_W, d), lambda i: (0, 0, 0), **const),
            pl.BlockSpec((d, d), lambda i: (0, 0), **const),
            pl.BlockSpec((4, d), lambda i: (0, 0)),
            pl.BlockSpec((None, 1, 6 * d), lambda i: ((i * tm) // rows_per_mod, 0, 0)),
            pl.BlockSpec((d, LANES), lambda i: (0, 0))],
        out_specs=[pl.BlockSpec((tm, d), lambda i: (i, 0)),
                   pl.BlockSpec((tm * H_SLABS, H_SLAB_W), lambda i: (i, 0)),
                   pl.BlockSpec((N_EXPERTS, tm), lambda i: (0, i))],
        out_shape=[jax.ShapeDtypeStruct((n, d), F32),
                   jax.ShapeDtypeStruct((n * H_SLABS, H_SLAB_W), jnp.uint32),
                   jax.ShapeDtypeStruct((N_EXPERTS, n), F32)],
        compiler_params=_params("arbitrary"),
        name="merge",
    )(x, *[a for a, _ in ys], gl, wb, wo, gains, mod_rows, wr)


CUMSUM_CHUNK = 256


def _lane_cumsum(x, upper):
    n = x.shape[1]
    carry = jnp.zeros((x.shape[0], 1), F32)
    out = []
    for c in range(n // CUMSUM_CHUNK):
        xc = x[:, c * CUMSUM_CHUNK:(c + 1) * CUMSUM_CHUNK]
        out.append(jnp.dot(xc.astype(BF16), upper, preferred_element_type=F32) + carry)
        carry = carry + jnp.sum(xc, axis=1, keepdims=True)
    return jnp.concatenate(out, axis=1) if len(out) > 1 else out[0]


def _route_kernel(a_ref, posm_ref, idx_ref, *, n, cap):
    bits = lax.bitcast_convert_type(a_ref[...], I32)
    thr = jnp.zeros((N_EXPERTS, 1), I32)
    for bit in range(30, -1, -1):
        cand = thr | (1 << bit)
        cnt = jnp.sum((bits >= cand).astype(F32), axis=1, keepdims=True)
        thr = jnp.where(cnt >= cap, cand, thr)
    gt = bits > thr
    eq = bits == thr
    need = cap - jnp.sum(gt.astype(F32), axis=1, keepdims=True)
    r = lax.broadcasted_iota(I32, (CUMSUM_CHUNK, CUMSUM_CHUNK), 0)
    c = lax.broadcasted_iota(I32, (CUMSUM_CHUNK, CUMSUM_CHUNK), 1)
    upper = (r <= c).astype(BF16)
    cum_eq = _lane_cumsum(eq.astype(F32), upper)
    sel = gt | (eq & (cum_eq <= need))
    pos = _lane_cumsum(sel.astype(F32), upper) - 1.0
    posm = jnp.where(sel, pos, -1.0).astype(I32)
    posm_ref[...] = posm
    tok = lax.broadcasted_iota(I32, (n, LANES), 0)
    lane = lax.broadcasted_iota(I32, (n, LANES), 1)
    tmat = jnp.where(lane == 0, tok // 64, jnp.where(lane == 1, tok % 64, 0)).astype(BF16)
    slot = lax.broadcasted_iota(I32, (cap, n), 0)
    for e in range(N_EXPERTS):
        onehot = (posm[e:e + 1, :] == slot).astype(BF16)
        idx_ref[e] = jnp.dot(onehot, tmat, preferred_element_type=F32)


def _route(aff_t, n_seq, n):
    cap = EC_CAPACITY_FACTOR * n // N_EXPERTS
    return pl.pallas_call(
        functools.partial(_route_kernel, n=n, cap=cap),
        grid=(n_seq,),
        in_specs=[pl.BlockSpec((N_EXPERTS, n), lambda b: (0, b))],
        out_specs=[pl.BlockSpec((N_EXPERTS, n), lambda b: (0, b)),
                   pl.BlockSpec((None, N_EXPERTS, cap, LANES), lambda b: (b, 0, 0, 0))],
        out_shape=[jax.ShapeDtypeStruct((N_EXPERTS, n_seq * n), I32),
                   jax.ShapeDtypeStruct((n_seq, N_EXPERTS, cap, LANES), F32)],
        compiler_params=_params("arbitrary"),
        name="route",
    )(aff_t)


COMBINE_TILE = 128
COMBINE_WIN = COMBINE_TILE + BF16_ROWS


def _expert_kernel(idx_ref, hc_ref, hl_ref, wg_ref, wu_ref, wd_ref, o_ref, xg, xb, hmid, sem,
                   *, mc, ml, nf):
    e = pl.program_id(0)
    f = pl.program_id(1)
    n_exp = pl.num_programs(0)
    m = mc + ml
    cc, cl = mc // nf, ml // nf

    def start_row(src_ref, expert, p):
        src = pl.multiple_of(idx_ref[expert * m + p] * H_SLABS, H_SLABS)
        dst = pl.multiple_of(p * H_SLABS, H_SLABS)
        pltpu.make_async_copy(src_ref.at[pl.ds(src, H_SLABS), :],
                              xg.at[pl.ds(dst, H_SLABS), :], sem).start()

    def wait_rows():
        pltpu.make_async_copy(hl_ref.at[pl.ds(0, m * H_SLABS), :], xg, sem).wait()

    @pl.when(f == 0)
    def _():
        @pl.when(e == 0)
        def _():
            def issue_ctx(p, carry):
                start_row(hc_ref, 0, p)
                return carry

            def issue_lat(p, carry):
                start_row(hl_ref, 0, mc + p)
                return carry

            lax.fori_loop(0, mc, issue_ctx, 0, unroll=8)
            lax.fori_loop(0, ml, issue_lat, 0, unroll=8)

        wait_rows()
        for s in range(H_SLABS):
            u = xg[pl.ds(s, m, stride=H_SLABS), :]
            even = lax.bitcast_convert_type(u << 16, F32)
            odd = lax.bitcast_convert_type(u & jnp.uint32(0xFFFF0000), F32)
            xb[:, (2 * s) * LANES:(2 * s + 1) * LANES] = even.astype(BF16)
            xb[:, (2 * s + 1) * LANES:(2 * s + 2) * LANES] = odd.astype(BF16)

    @pl.when(f < nf)
    def _():
        nxt = jnp.where(e + 1 < n_exp, e + 1, 0)
        for k in range(cc):
            start_row(hc_ref, nxt, f * cc + k)
        for k in range(cl):
            start_row(hl_ref, nxt, mc + f * cl + k)
        x = xb[...]
        a = jnp.dot(x, wg_ref[...].astype(BF16), preferred_element_type=F32)
        b = jnp.dot(x, wu_ref[...].astype(BF16), preferred_element_type=F32)
        hmid[f] = (a * jax.nn.sigmoid(a) * b).astype(BF16)

    @pl.when(f >= nf)
    def _():
        tf = hmid.shape[2]
        out = None
        for k in range(nf):
            part = jnp.dot(hmid[k], wd_ref[k * tf:(k + 1) * tf, :].astype(BF16),
                           preferred_element_type=F32)
            out = part if out is None else out + part
        o_ref[0:m, :] = out.astype(o_ref.dtype)
        o_ref[m:, :] = jnp.zeros((o_ref.shape[0] - m, o_ref.shape[1]), o_ref.dtype)

    @pl.when((e == n_exp - 1) & (f == pl.num_programs(1) - 1))
    def _():
        wait_rows()


def _expert_ffn(idx_flat, h_ctx, h_lat, w_gate, w_up, w_down, layer, mc, ml, tf, tn):
    _, n_exp, d, ff = w_gate.shape
    m = mc + ml
    m_pad = m + COMBINE_WIN
    nf = ff // tf
    assert mc % nf == 0 and ml % nf == 0 and h_lat.shape[0] >= m * H_SLABS
    grid_spec = pltpu.PrefetchScalarGridSpec(
        num_scalar_prefetch=1,
        grid=(n_exp, nf + d // tn),
        in_specs=[pl.BlockSpec(memory_space=pl.ANY),
                  pl.BlockSpec(memory_space=pl.ANY),
                  pl.BlockSpec((None, None, d, tf),
                               lambda e, f, idx: (layer, e, 0, jnp.minimum(f, nf - 1))),
                  pl.BlockSpec((None, None, d, tf),
                               lambda e, f, idx: (layer, e, 0, jnp.minimum(f, nf - 1))),
                  pl.BlockSpec((None, None, ff, tn),
                               lambda e, f, idx: (layer, e, 0, jnp.maximum(f - nf, 0)))],
        out_specs=pl.BlockSpec((None, m_pad, tn), lambda e, f, idx: (e, 0, jnp.maximum(f - nf, 0))),
        scratch_shapes=[pltpu.VMEM((m * H_SLABS, H_SLAB_W), jnp.uint32),
                        pltpu.VMEM((m, d), BF16),
                        pltpu.VMEM((nf, m, tf), BF16),
                        pltpu.SemaphoreType.DMA(())])
    return pl.pallas_call(
        functools.partial(_expert_kernel, mc=mc, ml=ml, nf=nf),
        grid_spec=grid_spec,
        out_shape=jax.ShapeDtypeStruct((n_exp, m_pad, d), BF16),
        compiler_params=_params("arbitrary", "arbitrary"),
        name="expert_ffn",
    )(idx_flat, h_ctx, h_lat, w_gate, w_up, w_down)


def _combine_kernel(w0_ref, x_ref, posm_ref, aff_ref, gain_ref, mod_ref, *rest):
    win_refs = rest[:N_EXPERTS]
    o_ref = rest[N_EXPERTS]
    t = pl.program_id(0)
    d = D_MODEL
    tt = COMBINE_TILE
    pad = jnp.zeros((LANES - 2 * N_EXPERTS, tt), F32)
    both = jnp.concatenate([posm_ref[...].astype(F32), aff_ref[...], pad], axis=0).T
    slot = lax.broadcasted_iota(I32, (tt, COMBINE_WIN), 1).astype(F32)
    ff = None
    for e in range(N_EXPERTS):
        rel = both[:, e:e + 1] - w0_ref[t * N_EXPERTS + e].astype(F32)
        weight = jnp.where(rel == slot, both[:, N_EXPERTS + e:N_EXPERTS + e + 1], 0.0).astype(BF16)
        c = jnp.dot(weight, win_refs[e][...], preferred_element_type=F32)
        ff = c if ff is None else ff + c
    o_ref[...] = x_ref[...] + mod_ref[:, 5 * d:6 * d] * _rms(ff, gain_ref[3:4, :])


def _combine(w0_flat, x1, posm, aff_t, gains, mod_rows, ys):
    n, d = x1.shape
    tt = COMBINE_TILE
    rows_per_mod = n // mod_rows.shape[0]

    def win_spec(e):
        return pl.BlockSpec(
            (None, pl.Element(COMBINE_WIN), pl.Element(d)),
            lambda t, w0: (e, pl.multiple_of(w0[t * N_EXPERTS + e], BF16_ROWS), 0))

    grid_spec = pltpu.PrefetchScalarGridSpec(
        num_scalar_prefetch=1,
        grid=(n // tt,),
        in_specs=[pl.BlockSpec((tt, d), lambda t, w0: (t, 0)),
                  pl.BlockSpec((N_EXPERTS, tt), lambda t, w0: (0, t)),
                  pl.BlockSpec((N_EXPERTS, tt), lambda t, w0: (0, t)),
                  pl.BlockSpec((4, d), lambda t, w0: (0, 0)),
                  pl.BlockSpec((None, 1, 6 * d), lambda t, w0: ((t * tt) // rows_per_mod, 0, 0))]
                 + [win_spec(e) for e in range(N_EXPERTS)],
        out_specs=pl.BlockSpec((tt, d), lambda t, w0: (t, 0)))
    return pl.pallas_call(
        _combine_kernel,
        grid_spec=grid_spec,
        out_shape=jax.ShapeDtypeStruct((n, d), F32),
        compiler_params=_params("arbitrary"),
        name="combine",
    )(w0_flat, x1, posm, aff_t, gains, mod_rows, *([ys] * N_EXPERTS))


def _slot_indices(idx_parts, n_seq, n):
    tok = (idx_parts[..., 0] * 64.0 + idx_parts[..., 1]).astype(I32)
    tok = tok + (jnp.arange(n_seq, dtype=I32) * n)[:, None, None]
    return jnp.transpose(tok, (1, 0, 2)).reshape(N_EXPERTS, -1)


def _window_starts(posm, n_seq, n, base):
    cap = EC_CAPACITY_FACTOR * n // N_EXPERTS
    tt = COMBINE_TILE
    sel = (posm >= 0).reshape(N_EXPERTS, n_seq, n // tt, tt)
    cnt = jnp.sum(sel.astype(I32), axis=-1)
    before = jnp.cumsum(cnt, axis=-1) - cnt
    start = before + (jnp.arange(n_seq, dtype=I32) * cap)[None, :, None] + base
    start = (start // BF16_ROWS) * BF16_ROWS
    return jnp.transpose(start.reshape(N_EXPERTS, -1), (1, 0)).reshape(-1)


def _global_positions(posm, n_seq, n, base):
    cap = EC_CAPACITY_FACTOR * n // N_EXPERTS
    offs = jnp.repeat(jnp.arange(n_seq, dtype=I32) * cap, n)[None, :] + base
    return jnp.where(posm >= 0, posm + offs, -1)


FFN_HIDDEN_TILE = 512
FFN_OUT_TILE = 256


def _channel_sublayers(ctx, lat, gains, w_gate, w_up, w_down, layer):
    routed = []
    for (x1, h, aff_t, mod_rows, n_seq, n) in (ctx, lat):
        posm, idx_parts = _route(aff_t, n_seq, n)
        routed.append((posm, _slot_indices(idx_parts, n_seq, n)))
    mc, ml = routed[0][1].shape[1], routed[1][1].shape[1]
    idx_flat = jnp.concatenate([routed[0][1], routed[1][1]], axis=1).reshape(-1)
    ys = _expert_ffn(idx_flat, ctx[1], lat[1], w_gate, w_up, w_down, layer, mc, ml,
                     FFN_HIDDEN_TILE, FFN_OUT_TILE)
    outs = []
    for (x1, h, aff_t, mod_rows, n_seq, n), (posm, _), base in zip((ctx, lat), routed, (0, mc)):
        outs.append(_combine(_window_starts(posm, n_seq, n, base), x1,
                             _global_positions(posm, n_seq, n, base), aff_t, gains, mod_rows, ys))
    return outs


def kernel(x_prompt, x_sample, c, cache_na_k, cache_na_v, cache_win_k, cache_win_v, c_ctx, w_ada, b_ada, norm_gain, w_in, na_rpb, win_sink, pool_w, pool_scale, w_branch, w_out, w_router, w_e_gate, w_e_up, w_e_down):
    batch, seq, d = x_prompt.shape
    dec_batch, dec_seq, _ = x_sample.shape
    depth = w_in.shape[0]
    past = cache_na_k.shape[2]
    assert d == D_MODEL and dec_seq % GRID_W == 0 and dec_batch <= SUBLANES - 1

    cvecs = jnp.zeros((SUBLANES, d), F32).at[0].set(c_ctx).at[1:1 + dec_batch].set(c)
    mods = _modulation(cvecs, w_ada, b_ada)

    cos_r, sin_r = _rope_tables(dec_seq)
    cos_ctx, sin_ctx = _dft_tables(seq)
    cos_lat, sin_lat = _dft_tables(dec_seq)
    cos_c, sin_c = _dft_tables(FNET_GROUP_W)

    xp = x_prompt.reshape(batch * seq, d)
    xs = x_sample.reshape(dec_batch * dec_seq, d)
    na_k, na_v, win_k, win_v = [], [], [], []
    for l in range(depth):
        gains = norm_gain[l]
        wb = w_branch[l].astype(BF16)
        wo = w_out[l].astype(BF16)
        wr = jnp.pad(w_router[l], ((0, 0), (0, LANES - N_EXPERTS)))
        mod_ctx = mods[l, 0:1].reshape(1, 1, 6 * d)
        mod_lat = mods[l, 1:1 + dec_batch].reshape(dec_batch, 1, 6 * d)

        ps, gl = _in_projection(xp, gains[0], mod_ctx, w_in, l)
        y_attn = _ctx_attention(ps, win_sink[l], batch, seq)
        y_pool = _pool_mix(ps, pool_w[l], pool_scale[l], batch, seq)
        y_fnet = _fourier_mix(ps, cos_ctx, sin_ctx, cos_c, sin_c, batch, seq)
        merged_ctx = _merge(xp, [(y_attn, 0), (y_pool, 0), (y_attn, 1), (y_fnet, 0)],
                            gl, wb, wo, gains, mod_ctx, wr)
        hd = HEAD_DIM
        na_k.append(ps[:, COL_NA_K * hd:(COL_NA_K + NA_HEADS) * hd])
        na_v.append(ps[:, COL_NA_V * hd:(COL_NA_V + NA_HEADS) * hd])
        win_k.append(ps[:, COL_WIN_K * hd:(COL_WIN_K + WIN_KV_HEADS) * hd])
        win_v.append(ps[:, COL_WIN_V * hd:(COL_WIN_V + WIN_KV_HEADS) * hd])

        ps, gl = _in_projection(xs, gains[0], mod_lat, w_in, l)
        table = _na_bias_table(na_rpb[l], dec_seq // GRID_W)
        y_na = _latent_na(ps, cache_na_k[:, l].reshape(dec_batch, past, NA_HEADS * hd),
                          cache_na_v[:, l].reshape(dec_batch, past, NA_HEADS * hd),
                          table, dec_batch, dec_seq)
        y_win = _latent_win(ps, cache_win_k[:, l].reshape(dec_batch, past, WIN_KV_HEADS * hd),
                            cache_win_v[:, l].reshape(dec_batch, past, WIN_KV_HEADS * hd),
                            win_sink[l], cos_r, sin_r, dec_batch, dec_seq)
        y_pool = _pool_mix(ps, pool_w[l], pool_scale[l], dec_batch, dec_seq)
        y_fnet = _fourier_mix(ps, cos_lat, sin_lat, cos_c, sin_c, dec_batch, dec_seq)
        merged_lat = _merge(xs, [(y_na, 0), (y_pool, 0), (y_win, 0), (y_fnet, 0)],
                            gl, wb, wo, gains, mod_lat, wr)

        xp, xs = _channel_sublayers((*merged_ctx, mod_ctx, batch, seq),
                                    (*merged_lat, mod_lat, dec_batch, dec_seq),
                                    gains, w_e_gate, w_e_up, w_e_down, l)

    def stack(parts, heads):
        wide = jnp.stack([p.reshape(batch, seq, heads * HEAD_DIM) for p in parts], axis=1)
        return wide.reshape(batch, depth, seq, heads, HEAD_DIM)

    return (xp.reshape(batch, seq, d), xs.reshape(dec_batch, dec_seq, d),
            stack(na_k, NA_HEADS), stack(na_v, NA_HEADS),
            stack(win_k, WIN_KV_HEADS), stack(win_v, WIN_KV_HEADS))
```
